```python
import jax, jax.numpy as jnp
from jax import lax
import numpy as np

D_MODEL = 1024
BATCH = 4
SEQ = 4096
DEPTH = 1

CHUNK = 64
N_META = 16
Q_BLOCK = 128
D_CONV = 512
CONV_WIDTH = 31
N_HEADS = 8
QK_NOPE = 64
QK_ROPE = 32
V_HEAD = 64
D_ATTN = N_HEADS * V_HEAD
Q_LORA = 384
KV_LORA = 256
ROPE_THETA = 10000.0
D_MIX = D_CONV + D_ATTN
D_IN = 2 * D_CONV + Q_LORA + KV_LORA + QK_ROPE
D_FF = 2816
FFN_CONV_WIDTH = 3
EPS = 1e-6
NEG = -1e30

kernel_name = 'hymba_conformer_mla_convffn_block'


def rms_norm(x, g):
    xf = x.astype(jnp.float32)
    y = xf * lax.rsqrt(jnp.mean(xf * xf, axis=-1, keepdims=True) + EPS)
    return (y * g.astype(jnp.float32)).astype(x.dtype)


def layer_norm(x, g, b):
    xf = x.astype(jnp.float32)
    mu = jnp.mean(xf, axis=-1, keepdims=True)
    var = jnp.mean(jnp.square(xf - mu), axis=-1, keepdims=True)
    y = (xf - mu) * lax.rsqrt(var + EPS)
    return (y * g.astype(jnp.float32) + b.astype(jnp.float32)).astype(x.dtype)


def causal_depthwise_conv(x, w, b):
    k = w.shape[0]
    y = lax.conv_general_dilated(
        x, w[:, None, :].astype(x.dtype), window_strides=(1,), padding=[(k - 1, 0)],
        dimension_numbers=('NWC', 'WIO', 'NWC'), feature_group_count=x.shape[-1])
    return y + b.astype(x.dtype)


def rope(x, cos, sin):
    half = x.shape[-1] // 2
    x1, x2 = x[..., :half], x[..., half:]
    return jnp.concatenate([x1 * cos - x2 * sin, x2 * cos + x1 * sin], axis=-1)


def block_causal_attention(q, k, v, chunk_id):
    b, l, h, dqk = q.shape
    dv = v.shape[-1]
    nblk = l // Q_BLOCK
    qb = q.reshape(b, nblk, Q_BLOCK, h, dqk).transpose(1, 0, 2, 3, 4)
    cb = chunk_id.reshape(nblk, Q_BLOCK)
    scale = dqk ** -0.5

    def one_block(args):
        qi, ci = args
        s = jnp.einsum('bqhd,bkhd->bhqk', qi, k, preferred_element_type=jnp.float32) * scale
        visible = ci[:, None] >= chunk_id[None, :]
        s = jnp.where(visible[None, None], s, NEG)
        p = jax.nn.softmax(s, axis=-1)
        return jnp.einsum('bhqk,bkhd->bqhd', p.astype(v.dtype), v)

    o = lax.map(one_block, (qb, cb))
    return o.transpose(1, 0, 2, 3, 4).reshape(b, l, h * dv)


def hybrid_layer(h, cos, sin, chunk_id, mix_norm_g, w_in, q_norm_g, w_uq, kv_norm_g, w_ukv,
                 conv_w, conv_b, conv_ln_g, conv_ln_b, conv_out_g, attn_out_g, w_out,
                 ffn_norm_g, w_ffn_up, ffn_conv_w, ffn_conv_b, w_ffn_down):
    b, l, _ = h.shape
    n = rms_norm(h, mix_norm_g)
    z = n @ w_in.astype(h.dtype)
    a, gate, c_q, c_kv, k_r = jnp.split(
        z, [D_CONV, 2 * D_CONV, 2 * D_CONV + Q_LORA, 2 * D_CONV + Q_LORA + KV_LORA], axis=-1)

    u = a * jax.nn.sigmoid(gate)
    u = causal_depthwise_conv(u, conv_w, conv_b)
    u = jax.nn.silu(layer_norm(u, conv_ln_g, conv_ln_b))

    q = (rms_norm(c_q, q_norm_g) @ w_uq.astype(h.dtype)).reshape(b, l, N_HEADS, QK_NOPE + QK_ROPE)
    kv = (rms_norm(c_kv, kv_norm_g) @ w_ukv.astype(h.dtype)).reshape(b, l, N_HEADS, QK_NOPE + V_HEAD)
    q_nope, q_rot = q[..., :QK_NOPE], q[..., QK_NOPE:]
    k_nope, v = kv[..., :QK_NOPE], kv[..., QK_NOPE:]
    q_rot = rope(q_rot, cos[:, None, :], sin[:, None, :])
    k_rot = rope(k_r, cos, sin)
    qf = jnp.concatenate([q_nope, q_rot], axis=-1)
    kf = jnp.concatenate(
        [k_nope, jnp.broadcast_to(k_rot[:, :, None, :], (b, l, N_HEADS, QK_ROPE))], axis=-1)
    o = block_causal_attention(qf, kf, v, chunk_id)

    mix = jnp.concatenate([rms_norm(u, conv_out_g), rms_norm(o, attn_out_g)], axis=-1)
    h = h + mix @ w_out.astype(h.dtype)

    n2 = rms_norm(h, ffn_norm_g)
    up = causal_depthwise_conv(n2 @ w_ffn_up.astype(h.dtype), ffn_conv_w, ffn_conv_b)
    g, val = up[..., :D_FF], up[..., D_FF:]
    return h + (jax.nn.silu(g) * val) @ w_ffn_down.astype(h.dtype)


def setup_inputs(seed: int = 0) -> dict:
    key = jax.random.key(seed)
    ks = jax.random.split(key, 24)
    f32 = jnp.float32

    def nrm(k, shape, scale):
        return jax.random.normal(k, shape, f32) * scale

    def gain(k, shape):
        return 1.0 + 0.02 * jax.random.normal(k, shape, f32)

    L = DEPTH
    return {
        'x': jax.random.normal(ks[0], (BATCH, SEQ, D_MODEL), f32),
        'meta_tokens': nrm(ks[1], (N_META, D_MODEL), 1.0),
        'mix_norm_g': gain(ks[2], (L, D_MODEL)),
        'w_in': nrm(ks[3], (L, D_MODEL, D_IN), D_MODEL ** -0.5),
        'q_norm_g': gain(ks[4], (L, Q_LORA)),
        'w_uq': nrm(ks[5], (L, Q_LORA, N_HEADS * (QK_NOPE + QK_ROPE)), Q_LORA ** -0.5),
        'kv_norm_g': gain(ks[6], (L, KV_LORA)),
        'w_ukv': nrm(ks[7], (L, KV_LORA, N_HEADS * (QK_NOPE + V_HEAD)), KV_LORA ** -0.5),
        'conv_w': nrm(ks[8], (L, CONV_WIDTH, D_CONV), CONV_WIDTH ** -0.5),
        'conv_b': nrm(ks[9], (L, D_CONV), 0.02),
        'conv_ln_g': gain(ks[10], (L, D_CONV)),
        'conv_ln_b': nrm(ks[11], (L, D_CONV), 0.02),
        'conv_out_g': gain(ks[12], (L, D_CONV)),
        'attn_out_g': gain(ks[13], (L, D_ATTN)),
        'w_out': nrm(ks[14], (L, D_MIX, D_MODEL), D_MIX ** -0.5),
        'ffn_norm_g': gain(ks[15], (L, D_MODEL)),
        'w_ffn_up': nrm(ks[16], (L, D_MODEL, 2 * D_FF), D_MODEL ** -0.5),
        'ffn_conv_w': nrm(ks[17], (L, FFN_CONV_WIDTH, 2 * D_FF), FFN_CONV_WIDTH ** -0.5),
        'ffn_conv_b': nrm(ks[18], (L, 2 * D_FF), 0.02),
        'w_ffn_down': nrm(ks[19], (L, D_FF, D_MODEL), D_FF ** -0.5),
        'final_norm_g': gain(ks[20], (D_MODEL,)),
    }


def reference(x, meta_tokens, mix_norm_g, w_in, q_norm_g, w_uq, kv_norm_g, w_ukv,
              conv_w, conv_b, conv_ln_g, conv_ln_b, conv_out_g, attn_out_g, w_out,
              ffn_norm_g, w_ffn_up, ffn_conv_w, ffn_conv_b, w_ffn_down, final_norm_g):
    b, s, d = x.shape
    l_real = N_META + s
    l_pad = ((l_real + Q_BLOCK - 1) // Q_BLOCK) * Q_BLOCK
    meta = jnp.broadcast_to(meta_tokens[None].astype(x.dtype), (b, N_META, d))
    pad = jnp.zeros((b, l_pad - l_real, d), x.dtype)
    h = jnp.concatenate([meta, x, pad], axis=1)

    pos = jnp.arange(l_pad, dtype=jnp.int32)
    chunk_id = jnp.where(pos < N_META, 0, 1 + (pos - N_META) // CHUNK).astype(jnp.int32)
    inv_freq = 1.0 / (ROPE_THETA ** (jnp.arange(0, QK_ROPE, 2, dtype=jnp.float32) / QK_ROPE))
    ang = pos.astype(jnp.float32)[:, None] * inv_freq[None, :]
    cos = jnp.cos(ang).astype(x.dtype)
    sin = jnp.sin(ang).astype(x.dtype)

    for i in range(DEPTH):
        h = hybrid_layer(h, cos, sin, chunk_id, mix_norm_g[i], w_in[i], q_norm_g[i], w_uq[i],
                         kv_norm_g[i], w_ukv[i], conv_w[i], conv_b[i], conv_ln_g[i], conv_ln_b[i],
                         conv_out_g[i], attn_out_g[i], w_out[i], ffn_norm_g[i], w_ffn_up[i],
                         ffn_conv_w[i], ffn_conv_b[i], w_ffn_down[i])

    h = rms_norm(h, final_norm_g)
    return h[:, N_META:N_META + s]
```

```python
import functools

import jax
import jax.numpy as jnp
from jax import lax
from jax.experimental import pallas as pl
from jax.experimental.pallas import tpu as pltpu

D_MODEL = 1024
CHUNK = 64
N_META = 16
D_CONV = 512
CONV_WIDTH = 31
N_HEADS = 8
QK_NOPE = 64
QK_ROPE = 32
V_HEAD = 64
D_ATTN = N_HEADS * V_HEAD
Q_LORA = 384
KV_LORA = 256
ROPE_THETA = 10000.0
D_FF = 2816
FFN_CONV_WIDTH = 3
EPS = 1e-6
NEG = -1e30

LANES = 128
SUBLANES = 8
T = 256
FRONT = T
HEAD_PAD = 128
HALO = 32
ROPE_HALF = QK_ROPE // 2
D_IN_PAD = 2 * D_CONV + Q_LORA + KV_LORA + LANES
KR_OFF = 2 * D_CONV + Q_LORA + KV_LORA
FF_CHUNK = 256
N_FF_CHUNKS = D_FF // FF_CHUNK
VMEM_LIMIT = 56 * 1024 * 1024


def _rms(x, g):
    return x * lax.rsqrt(jnp.mean(x * x, axis=-1, keepdims=True) + EPS) * g


def _tile_input(t, x_ref, meta_ref):
    h0 = jnp.concatenate(
        [jnp.zeros((T - N_META, D_MODEL), jnp.float32), meta_ref[...]], axis=0)
    return jnp.where(t == 0, h0, x_ref[0])


def _proj_kernel(x_ref, meta_ref, mix_g_ref, w_in_ref, q_g_ref, w_uqt_ref, kv_g_ref,
                 w_uk_ref, w_uvt_ref, cw_ref, cb_ref, ln_g_ref, ln_b_ref, cog_ref,
                 kc_ref, ks1_ref, ks2_ref, qcos_ref, qsin_ref,
                 u_ref, k_ref, qt_ref, vt_ref, xbuf):
    t = pl.program_id(1)
    h = _tile_input(t, x_ref, meta_ref)
    n = _rms(h, mix_g_ref[...]).astype(jnp.bfloat16)
    z = jnp.dot(n, w_in_ref[...], preferred_element_type=jnp.float32)

    a = z[:, :D_CONV]
    gate = z[:, D_CONV:2 * D_CONV]
    row = lax.broadcasted_iota(jnp.int32, (T, 1), 0) + t * T
    u = jnp.where(row >= FRONT - N_META, a * jax.nn.sigmoid(gate), 0.0)

    @pl.when(t == 0)
    def _():
        xbuf[:, 0:HALO, :] = jnp.zeros((D_CONV // LANES, HALO, LANES), jnp.float32)

    conv_slabs = []
    for s in range(D_CONV // LANES):
        xbuf[s, HALO:HALO + T, :] = u[:, s * LANES:(s + 1) * LANES]
        acc = jnp.zeros((T, LANES), jnp.float32) + cb_ref[:, s * LANES:(s + 1) * LANES]
        for k in range(CONV_WIDTH):
            off = HALO - (CONV_WIDTH - 1) + k
            acc = acc + cw_ref[k:k + 1, s * LANES:(s + 1) * LANES] * xbuf[s, off:off + T, :]
        conv_slabs.append(acc)
        xbuf[s, 0:HALO, :] = xbuf[s, T:T + HALO, :]
    c = jnp.concatenate(conv_slabs, axis=-1)
    mu = jnp.mean(c, axis=-1, keepdims=True)
    var = jnp.mean(jnp.square(c - mu), axis=-1, keepdims=True)
    c = (c - mu) * lax.rsqrt(var + EPS) * ln_g_ref[...] + ln_b_ref[...]
    c = c * jax.nn.sigmoid(c)
    u_ref[0] = _rms(c, cog_ref[...]).astype(u_ref.dtype)

    c_q = z[:, 2 * D_CONV:2 * D_CONV + Q_LORA]
    c_kv = z[:, 2 * D_CONV + Q_LORA:KR_OFF]
    k_r = z[:, KR_OFF:KR_OFF + LANES]
    qn = _rms(c_q, q_g_ref[...]).astype(jnp.bfloat16)
    kvn = _rms(c_kv, kv_g_ref[...]).astype(jnp.bfloat16)

    k_rot = (k_r * kc_ref[...]
             + pltpu.roll(k_r, ROPE_HALF, 1) * ks1_ref[...]
             + pltpu.roll(k_r, LANES - ROPE_HALF, 1) * ks2_ref[...])
    k_nope = jnp.dot(kvn, w_uk_ref[...], preferred_element_type=jnp.float32)
    for hd in range(N_HEADS):
        sl = slice(hd * HEAD_PAD, (hd + 1) * HEAD_PAD)
        k_ref[0, 0, :, sl] = (k_nope[:, sl] + k_rot).astype(k_ref.dtype)

    nt = (((1,), (1,)), ((), ()))
    vt_ref[0, 0] = lax.dot_general(w_uvt_ref[...], kvn, nt,
                                   preferred_element_type=jnp.float32).astype(vt_ref.dtype)
    qt = lax.dot_general(w_uqt_ref[...], qn, nt, preferred_element_type=jnp.float32)
    scale = (QK_NOPE + QK_ROPE) ** -0.5
    cs = qcos_ref[...]
    sn = qsin_ref[...]
    for hd in range(N_HEADS):
        b0 = hd * HEAD_PAD
        qt_ref[0, 0, b0:b0 + QK_NOPE, :] = (qt[b0:b0 + QK_NOPE] * scale).astype(qt_ref.dtype)
        x1 = qt[b0 + QK_NOPE:b0 + QK_NOPE + ROPE_HALF]
        x2 = qt[b0 + QK_NOPE + ROPE_HALF:b0 + QK_NOPE + QK_ROPE]
        qt_ref[0, 0, b0 + QK_NOPE:b0 + QK_NOPE + ROPE_HALF, :] = (
            x1 * cs - x2 * sn).astype(qt_ref.dtype)
        qt_ref[0, 0, b0 + QK_NOPE + ROPE_HALF:b0 + QK_NOPE + QK_ROPE, :] = (
            x2 * cs + x1 * sn).astype(qt_ref.dtype)
        qt_ref[0, 0, b0 + QK_NOPE + QK_ROPE:b0 + HEAD_PAD, :] = jnp.zeros(
            (HEAD_PAD - QK_NOPE - QK_ROPE, T), qt_ref.dtype)


def _attn_kernel(qt_ref, k_ref, vt_ref, km_ref, vm_ref, g_ref, o_ref, ot_sc):
    i = pl.program_id(1)
    key_chunk = lax.broadcasted_iota(jnp.int32, (T, T), 0) // CHUNK
    qry_chunk = lax.broadcasted_iota(jnp.int32, (T, T), 1) // CHUNK
    diag_visible = key_chunk <= qry_chunk

    for hd in range(N_HEADS):
        qsl = slice(hd * HEAD_PAD, (hd + 1) * HEAD_PAD)
        vsl = slice(hd * V_HEAD, (hd + 1) * V_HEAD)
        qth = qt_ref[0, 0, qsl, :]

        s = jnp.dot(km_ref[0, :, qsl], qth, preferred_element_type=jnp.float32)
        m = jnp.max(s, axis=0, keepdims=True)
        p = jnp.exp(s - m)
        l = jnp.sum(p, axis=0, keepdims=True)
        p_pad = jnp.concatenate(
            [p.astype(jnp.bfloat16), jnp.zeros((LANES - N_META, T), jnp.bfloat16)], axis=0)
        acc = jnp.dot(vm_ref[0, vsl, :], p_pad, preferred_element_type=jnp.float32)

        def step(j, carry, masked):
            m, l, acc = carry
            s = jnp.dot(k_ref[0, j, :, qsl], qth, preferred_element_type=jnp.float32)
            if masked:
                s = jnp.where(diag_visible, s, NEG)
            m_new = jnp.maximum(m, jnp.max(s, axis=0, keepdims=True))
            alpha = jnp.exp(m - m_new)
            p = jnp.exp(s - m_new)
            l = alpha * l + jnp.sum(p, axis=0, keepdims=True)
            acc = alpha * acc + jnp.dot(vt_ref[0, j, vsl, :], p.astype(jnp.bfloat16),
                                        preferred_element_type=jnp.float32)
            return m_new, l, acc

        carry = lax.fori_loop(1, i, functools.partial(step, masked=False), (m, l, acc))
        diag = step(i, carry, masked=True)
        m, l, acc = [jnp.where(i > 0, d, c) for d, c in zip(diag, carry)]
        ot_sc[vsl, :] = acc / l

    ot = ot_sc[...]
    ot = ot * lax.rsqrt(jnp.mean(ot * ot, axis=0, keepdims=True) + EPS)
    o_ref[0] = (ot.T * g_ref[...]).astype(o_ref.dtype)


def _ffn_kernel(x_ref, meta_ref, u_ref, o_ref, w_out_ref, ffn_g_ref, w_up_ref, fcw_ref,
                fcb_ref, w_down_ref, fin_g_ref, out_ref, upbuf, acc_ref):
    t = pl.program_id(1)
    h = _tile_input(t, x_ref, meta_ref)
    h1 = (h
          + jnp.dot(u_ref[0], w_out_ref[0:D_CONV, :], preferred_element_type=jnp.float32)
          + jnp.dot(o_ref[0], w_out_ref[D_CONV:, :], preferred_element_type=jnp.float32))
    acc_ref[...] = h1
    n2 = _rms(h1, ffn_g_ref[...]).astype(jnp.bfloat16)
    row = lax.broadcasted_iota(jnp.int32, (T, 1), 0) + t * T
    valid = row >= FRONT - N_META

    @pl.when(t == 0)
    def _():
        upbuf[:, 0:SUBLANES, :] = jnp.zeros((2 * D_FF // LANES, SUBLANES, LANES), jnp.float32)

    def conv3(col0):
        up = jnp.dot(n2, w_up_ref[:, col0:col0 + FF_CHUNK], preferred_element_type=jnp.float32)
        up = jnp.where(valid, up, 0.0)
        outs = []
        for s in range(FF_CHUNK // LANES):
            slab = col0 // LANES + s
            cols = slice(col0 + s * LANES, col0 + (s + 1) * LANES)
            upbuf[slab, SUBLANES:SUBLANES + T, :] = up[:, s * LANES:(s + 1) * LANES]
            y = fcb_ref[:, cols] + jnp.zeros((T, LANES), jnp.float32)
            for k in range(FFN_CONV_WIDTH):
                off = SUBLANES - (FFN_CONV_WIDTH - 1) + k
                y = y + fcw_ref[k:k + 1, cols] * upbuf[slab, off:off + T, :]
            outs.append(y)
            upbuf[slab, 0:SUBLANES, :] = upbuf[slab, T:T + SUBLANES, :]
        return jnp.concatenate(outs, axis=-1)

    for c in range(N_FF_CHUNKS):
        g = conv3(c * FF_CHUNK)
        val = conv3(D_FF + c * FF_CHUNK)
        act = (g * jax.nn.sigmoid(g) * val).astype(jnp.bfloat16)
        contrib = jnp.dot(act, w_down_ref[c * FF_CHUNK:(c + 1) * FF_CHUNK, :],
                          preferred_element_type=jnp.float32)
        acc_ref[...] += contrib
    out_ref[0] = _rms(acc_ref[...], fin_g_ref[...]).astype(out_ref.dtype)


def _full(shape):
    return pl.BlockSpec(shape, lambda b, t: (0,) * len(shape))


def kernel(x, meta_tokens, mix_norm_g, w_in, q_norm_g, w_uq, kv_norm_g, w_ukv, conv_w, conv_b,
           conv_ln_g, conv_ln_b, conv_out_g, attn_out_g, w_out, ffn_norm_g, w_ffn_up,
           ffn_conv_w, ffn_conv_b, w_ffn_down, final_norm_g):
    B, S, D = x.shape
    assert D == D_MODEL and S % T == 0 and mix_norm_g.shape[0] == 1
    nt = S // T + 1
    lp = nt * T
    bf16, f32 = jnp.bfloat16, jnp.float32
    row2 = lambda v: v.reshape(1, -1).astype(f32)

    w_in0 = w_in[0]
    w_in_p = jnp.concatenate([
        w_in0[:, :KR_OFF],
        jnp.zeros((D, QK_NOPE), f32), w_in0[:, KR_OFF:], jnp.zeros((D, LANES - QK_NOPE - QK_ROPE), f32),
    ], axis=1).astype(bf16)
    w_uq3 = w_uq[0].reshape(Q_LORA, N_HEADS, QK_NOPE + QK_ROPE)
    w_uqt = jnp.pad(w_uq3, ((0, 0), (0, 0), (0, HEAD_PAD - QK_NOPE - QK_ROPE))).reshape(
        Q_LORA, N_HEADS * HEAD_PAD).T.astype(bf16)
    w_ukv3 = w_ukv[0].reshape(KV_LORA, N_HEADS, QK_NOPE + V_HEAD)
    w_uk = jnp.pad(w_ukv3[:, :, :QK_NOPE], ((0, 0), (0, 0), (0, HEAD_PAD - QK_NOPE))).reshape(
        KV_LORA, N_HEADS * HEAD_PAD).astype(bf16)
    w_uvt = w_ukv3[:, :, QK_NOPE:].reshape(KV_LORA, D_ATTN).T.astype(bf16)

    pos = (jnp.arange(lp, dtype=jnp.int32) - (FRONT - N_META)).astype(f32)
    inv_freq = 1.0 / (ROPE_THETA ** (jnp.arange(0, QK_ROPE, 2, dtype=f32) / QK_ROPE))
    ang = pos[:, None] * inv_freq[None, :]
    cos, sin = jnp.cos(ang), jnp.sin(ang)
    zl = lambda n: jnp.zeros((lp, n), f32)
    kc = jnp.concatenate([zl(QK_NOPE), cos, cos, zl(LANES - QK_NOPE - QK_ROPE)], axis=1)
    ks1 = jnp.concatenate([zl(QK_NOPE + ROPE_HALF), sin, zl(LANES - QK_NOPE - QK_ROPE)], axis=1)
    ks2 = jnp.concatenate([zl(QK_NOPE), -sin, zl(LANES - QK_NOPE - ROPE_HALF)], axis=1)
    scale = (QK_NOPE + QK_ROPE) ** -0.5
    qcos = (cos * scale).T
    qsin = (sin * scale).T

    x_spec = pl.BlockSpec((1, T, D), lambda b, t: (b, jnp.maximum(t - 1, 0), 0))
    params = pltpu.CompilerParams(dimension_semantics=("arbitrary", "arbitrary"),
                                  vmem_limit_bytes=VMEM_LIMIT)

    u_n, k4, qt4, vt4 = pl.pallas_call(
        _proj_kernel,
        grid=(B, nt),
        in_specs=[
            x_spec, _full((N_META, D)), _full((1, D)), _full((D, D_IN_PAD)),
            _full((1, Q_LORA)), _full((N_HEADS * HEAD_PAD, Q_LORA)), _full((1, KV_LORA)),
            _full((KV_LORA, N_HEADS * HEAD_PAD)), _full((D_ATTN, KV_LORA)),
            _full((CONV_WIDTH, D_CONV)), _full((1, D_CONV)), _full((1, D_CONV)),
            _full((1, D_CONV)), _full((1, D_CONV)),
            pl.BlockSpec((T, LANES), lambda b, t: (t, 0)),
            pl.BlockSpec((T, LANES), lambda b, t: (t, 0)),
            pl.BlockSpec((T, LANES), lambda b, t: (t, 0)),
            pl.BlockSpec((ROPE_HALF, T), lambda b, t: (0, t)),
            pl.BlockSpec((ROPE_HALF, T), lambda b, t: (0, t)),
        ],
        out_specs=[
            pl.BlockSpec((1, T, D_CONV), lambda b, t: (b, t, 0)),
            pl.BlockSpec((1, 1, T, N_HEADS * HEAD_PAD), lambda b, t: (b, t, 0, 0)),
            pl.BlockSpec((1, 1, N_HEADS * HEAD_PAD, T), lambda b, t: (b, t, 0, 0)),
            pl.BlockSpec((1, 1, D_ATTN, T), lambda b, t: (b, t, 0, 0)),
        ],
        out_shape=[
            jax.ShapeDtypeStruct((B, lp, D_CONV), bf16),
            jax.ShapeDtypeStruct((B, nt, T, N_HEADS * HEAD_PAD), bf16),
            jax.ShapeDtypeStruct((B, nt, N_HEADS * HEAD_PAD, T), bf16),
            jax.ShapeDtypeStruct((B, nt, D_ATTN, T), bf16),
        ],
        scratch_shapes=[pltpu.VMEM((D_CONV // LANES, HALO + T, LANES), f32)],
        compiler_params=params,
        name="proj_conv_qkv",
    )(x, meta_tokens.astype(f32), row2(mix_norm_g[0]), w_in_p, row2(q_norm_g[0]), w_uqt,
      row2(kv_norm_g[0]), w_uk, w_uvt, conv_w[0].astype(f32), row2(conv_b[0]),
      row2(conv_ln_g[0]), row2(conv_ln_b[0]), row2(conv_out_g[0]), kc, ks1, ks2, qcos, qsin)

    k_meta = k4[:, 0, T - N_META:, :]
    v_meta = jnp.pad(vt4[:, 0, :, T - N_META:],
                     ((0, 0), (0, 0), (0, LANES - N_META)))

    o_n = pl.pallas_call(
        _attn_kernel,
        grid=(B, nt),
        in_specs=[
            pl.BlockSpec((1, 1, N_HEADS * HEAD_PAD, T), lambda b, i: (b, i, 0, 0)),
            pl.BlockSpec((1, nt, T, N_HEADS * HEAD_PAD), lambda b, i: (b, 0, 0, 0)),
            pl.BlockSpec((1, nt, D_ATTN, T), lambda b, i: (b, 0, 0, 0)),
            pl.BlockSpec((1, N_META, N_HEADS * HEAD_PAD), lambda b, i: (b, 0, 0)),
            pl.BlockSpec((1, D_ATTN, LANES), lambda b, i: (b, 0, 0)),
            _full((1, D_ATTN)),
        ],
        out_specs=pl.BlockSpec((1, T, D_ATTN), lambda b, i: (b, i, 0)),
        out_shape=jax.ShapeDtypeStruct((B, lp, D_ATTN), bf16),
        scratch_shapes=[pltpu.VMEM((D_ATTN, T), f32)],
        compiler_params=params,
        name="block_causal_attn",
    )(qt4, k4, vt4, k_meta, v_meta, row2(attn_out_g[0]))

    out = pl.pallas_call(
        _ffn_kernel,
        grid=(B, nt),
        in_specs=[
            x_spec, _full((N_META, D)),
            pl.BlockSpec((1, T, D_CONV), lambda b, t: (b, t, 0)),
            pl.BlockSpec((1, T, D_ATTN), lambda b, t: (b, t, 0)),
            _full((D_CONV + D_ATTN, D)), _full((1, D)), _full((D, 2 * D_FF)),
            _full((FFN_CONV_WIDTH, 2 * D_FF)), _full((1, 2 * D_FF)), _full((D_FF, D)),
            _full((1, D)),
        ],
        out_specs=pl.BlockSpec((1, T, D), lambda b, t: (b, jnp.maximum(t - 1, 0), 0)),
        out_shape=jax.ShapeDtypeStruct((B, S, D), x.dtype),
        scratch_shapes=[pltpu.VMEM((2 * D_FF // LANES, SUBLANES + T, LANES), f32),
                        pltpu.VMEM((T, D), f32)],
        compiler_params=params,
        name="outproj_convffn",
    )(x, meta_tokens.astype(f32), u_n, o_n, w_out[0].astype(bf16), row2(ffn_norm_g[0]),
      w_ffn_up[0].astype(bf16), ffn_conv_w[0].astype(f32), row2(ffn_conv_b[0]),
      w_ffn_down[0].astype(bf16), row2(final_norm_g))
    return out
```

```python
import functools

import jax
import jax.numpy as jnp
from jax import lax
from jax.experimental import pallas as pl
from jax.experimental.pallas import tpu as pltpu

D_MODEL = 1024
CHUNK = 64
N_META = 16
D_CONV = 512
CONV_WIDTH = 31
N_HEADS = 8
QK_NOPE = 64
QK_ROPE = 32
V_HEAD = 64
D_ATTN = N_HEADS * V_HEAD
Q_LORA = 384
KV_LORA = 256
ROPE_THETA = 10000.0
D_FF = 2816
FFN_CONV_WIDTH = 3
EPS = 1e-6
NEG = -1e30

LANES = 128
SUBLANES = 8
T = 256
FRONT = T
HEAD_PAD = 128
HALO = 32
ROPE_HALF = QK_ROPE // 2
D_IN_PAD = 2 * D_CONV + Q_LORA + KV_LORA + LANES
KR_OFF = 2 * D_CONV + Q_LORA + KV_LORA
FF_CHUNK = 256
N_FF_CHUNKS = D_FF // FF_CHUNK
VMEM_LIMIT = 56 * 1024 * 1024


def _rms(x, g):
    return x * lax.rsqrt(jnp.mean(x * x, axis=-1, keepdims=True) + EPS) * g


def _tile_input(t, x_ref, meta_ref):
    h0 = jnp.concatenate(
        [jnp.zeros((T - N_META, D_MODEL), jnp.float32), meta_ref[...]], axis=0)
    return jnp.where(t == 0, h0, x_ref[0])


def _proj_kernel(x_ref, meta_ref, mix_g_ref, w_in_ref, q_g_ref, w_uqt_ref, kv_g_ref,
                 w_uk_ref, w_uvt_ref, cw_ref, cb_ref, ln_g_ref, ln_b_ref, cog_ref,
                 kc_ref, ks1_ref, ks2_ref, qcos_ref, qsin_ref,
                 u_ref, k_ref, qt_ref, vt_ref, xbuf):
    t = pl.program_id(1)
    h = _tile_input(t, x_ref, meta_ref)
    n = _rms(h, mix_g_ref[...]).astype(jnp.bfloat16)
    z = jnp.dot(n, w_in_ref[...], preferred_element_type=jnp.float32)

    a = z[:, :D_CONV]
    gate = z[:, D_CONV:2 * D_CONV]
    row = lax.broadcasted_iota(jnp.int32, (T, 1), 0) + t * T
    u = jnp.where(row >= FRONT - N_META, a * jax.nn.sigmoid(gate), 0.0)

    @pl.when(t == 0)
    def _():
        xbuf[:, 0:HALO, :] = jnp.zeros((D_CONV // LANES, HALO, LANES), jnp.float32)

    conv_slabs = []
    for s in range(D_CONV // LANES):
        xbuf[s, HALO:HALO + T, :] = u[:, s * LANES:(s + 1) * LANES]
        acc = jnp.zeros((T, LANES), jnp.float32) + cb_ref[:, s * LANES:(s + 1) * LANES]
        for k in range(CONV_WIDTH):
            off = HALO - (CONV_WIDTH - 1) + k
            acc = acc + cw_ref[k:k + 1, s * LANES:(s + 1) * LANES] * xbuf[s, off:off + T, :]
        conv_slabs.append(acc)
        xbuf[s, 0:HALO, :] = xbuf[s, T:T + HALO, :]
    c = jnp.concatenate(conv_slabs, axis=-1)
    mu = jnp.mean(c, axis=-1, keepdims=True)
    var = jnp.mean(jnp.square(c - mu), axis=-1, keepdims=True)
    c = (c - mu) * lax.rsqrt(var + EPS) * ln_g_ref[...] + ln_b_ref[...]
    c = c * jax.nn.sigmoid(c)
    u_ref[0] = _rms(c, cog_ref[...]).astype(u_ref.dtype)

    c_q = z[:, 2 * D_CONV:2 * D_CONV + Q_LORA]
    c_kv = z[:, 2 * D_CONV + Q_LORA:KR_OFF]
    k_r = z[:, KR_OFF:KR_OFF + LANES]
    qn = _rms(c_q, q_g_ref[...]).astype(jnp.bfloat16)
    kvn = _rms(c_kv, kv_g_ref[...]).astype(jnp.bfloat16)

    k_rot = (k_r * kc_ref[...]
             + pltpu.roll(k_r, ROPE_HALF, 1) * ks1_ref[...]
             + pltpu.roll(k_r, LANES - ROPE_HALF, 1) * ks2_ref[...])
    k_nope = jnp.dot(kvn, w_uk_ref[...], preferred_element_type=jnp.float32)
    for hd in range(N_HEADS):
        sl = slice(hd * HEAD_PAD, (hd + 1) * HEAD_PAD)
        k_ref[0, 0, :, sl] = (k_nope[:, sl] + k_rot).astype(k_ref.dtype)

    nt = (((1,), (1,)), ((), ()))
    vt_ref[0, 0] = lax.dot_general(w_uvt_ref[...], kvn, nt,
                                   preferred_element_type=jnp.float32).astype(vt_ref.dtype)
    qt = lax.dot_general(w_uqt_ref[...], qn, nt, preferred_element_type=jnp.float32)
    scale = (QK_NOPE + QK_ROPE) ** -0.5
    cs = qcos_ref[...]
    sn = qsin_ref[...]
    for hd in range(N_HEADS):
        b0 = hd * HEAD_PAD
        qt_ref[0, 0, b0:b0 + QK_NOPE, :] = (qt[b0:b0 + QK_NOPE] * scale).astype(qt_ref.dtype)
        x1 = qt[b0 + QK_NOPE:b0 + QK_NOPE + ROPE_HALF]
        x2 = qt[b0 + QK_NOPE + ROPE_HALF:b0 + QK_NOPE + QK_ROPE]
        qt_ref[0, 0, b0 + QK_NOPE:b0 + QK_NOPE + ROPE_HALF, :] = (
            x1 * cs - x2 * sn).astype(qt_ref.dtype)
        qt_ref[0, 0, b0 + QK_NOPE + ROPE_HALF:b0 + QK_NOPE + QK_ROPE, :] = (
            x2 * cs + x1 * sn).astype(qt_ref.dtype)
        qt_ref[0, 0, b0 + QK_NOPE + QK_ROPE:b0 + HEAD_PAD, :] = jnp.zeros(
            (HEAD_PAD - QK_NOPE - QK_ROPE, T), qt_ref.dtype)


def _attn_kernel(qt_ref, k_ref, vt_ref, km_ref, vm_ref, g_ref, o_ref, m_sc, l_sc, acc_sc):
    i = pl.program_id(1)
    heads = [(slice(hd * HEAD_PAD, (hd + 1) * HEAD_PAD), slice(hd * V_HEAD, (hd + 1) * V_HEAD))
             for hd in range(N_HEADS)]

    for hd, (qsl, vsl) in enumerate(heads):
        s = jnp.dot(km_ref[0, :, qsl], qt_ref[0, 0, qsl, :],
                    preferred_element_type=jnp.float32)
        m = jnp.max(s, axis=0, keepdims=True)
        p = jnp.exp(s - m)
        p_pad = jnp.concatenate(
            [p.astype(jnp.bfloat16), jnp.zeros((LANES - N_META, T), jnp.bfloat16)], axis=0)
        m_sc[hd] = m
        l_sc[hd] = jnp.sum(p, axis=0, keepdims=True)
        acc_sc[vsl, :] = jnp.dot(vm_ref[0, vsl, :], p_pad, preferred_element_type=jnp.float32)

    def tile_step(j, masked):
        if masked:
            key_chunk = lax.broadcasted_iota(jnp.int32, (T, T), 0) // CHUNK
            qry_chunk = lax.broadcasted_iota(jnp.int32, (T, T), 1) // CHUNK
            visible = key_chunk <= qry_chunk
        scores = [jnp.dot(k_ref[0, j, :, qsl], qt_ref[0, 0, qsl, :],
                          preferred_element_type=jnp.float32)
                  for qsl, _ in heads]
        probs, alphas = [], []
        for hd, s in enumerate(scores):
            if masked:
                s = jnp.where(visible, s, NEG)
            m = m_sc[hd]
            m_new = jnp.maximum(m, jnp.max(s, axis=0, keepdims=True))
            alpha = jnp.exp(m - m_new)
            p = jnp.exp(s - m_new)
            m_sc[hd] = m_new
            l_sc[hd] = alpha * l_sc[hd] + jnp.sum(p, axis=0, keepdims=True)
            probs.append(p.astype(jnp.bfloat16))
            alphas.append(alpha)
        for (_, vsl), p, alpha in zip(heads, probs, alphas):
            acc_sc[vsl, :] = alpha * acc_sc[vsl, :] + jnp.dot(
                vt_ref[0, j, vsl, :], p, preferred_element_type=jnp.float32)

    def full_tile(j, carry):
        tile_step(j, masked=False)
        return carry

    lax.fori_loop(1, i, full_tile, 0)

    @pl.when(i > 0)
    def _():
        tile_step(i, masked=True)

    for hd, (qsl, vsl) in enumerate(heads):
        acc_sc[vsl, :] = acc_sc[vsl, :] / l_sc[hd]
    ot = acc_sc[...]
    ot = ot * lax.rsqrt(jnp.mean(ot * ot, axis=0, keepdims=True) + EPS)
    o_ref[0] = (ot.T * g_ref[...]).astype(o_ref.dtype)


def _ffn_kernel(x_ref, meta_ref, u_ref, o_ref, w_out_ref, ffn_g_ref, w_up_ref, fcw_ref,
                fcb_ref, w_down_ref, fin_g_ref, out_ref, upbuf, acc_ref):
    t = pl.program_id(1)
    h = _tile_input(t, x_ref, meta_ref)
    h1 = (h
          + jnp.dot(u_ref[0], w_out_ref[0:D_CONV, :], preferred_element_type=jnp.float32)
          + jnp.dot(o_ref[0], w_out_ref[D_CONV:, :], preferred_element_type=jnp.float32))
    acc_ref[...] = h1
    n2 = _rms(h1, ffn_g_ref[...]).astype(jnp.bfloat16)
    row = lax.broadcasted_iota(jnp.int32, (T, 1), 0) + t * T
    valid = row >= FRONT - N_META

    @pl.when(t == 0)
    def _():
        upbuf[:, 0:SUBLANES, :] = jnp.zeros((2 * D_FF // LANES, SUBLANES, LANES), jnp.float32)

    def conv3(col0):
        up = jnp.dot(n2, w_up_ref[:, col0:col0 + FF_CHUNK], preferred_element_type=jnp.float32)
        up = jnp.where(valid, up, 0.0)
        outs = []
        for s in range(FF_CHUNK // LANES):
            slab = col0 // LANES + s
            cols = slice(col0 + s * LANES, col0 + (s + 1) * LANES)
            upbuf[slab, SUBLANES:SUBLANES + T, :] = up[:, s * LANES:(s + 1) * LANES]
            y = fcb_ref[:, cols] + jnp.zeros((T, LANES), jnp.float32)
            for k in range(FFN_CONV_WIDTH):
                off = SUBLANES - (FFN_CONV_WIDTH - 1) + k
                y = y + fcw_ref[k:k + 1, cols] * upbuf[slab, off:off + T, :]
            outs.append(y)
            upbuf[slab, 0:SUBLANES, :] = upbuf[slab, T:T + SUBLANES, :]
        return jnp.concatenate(outs, axis=-1)

    for c in range(N_FF_CHUNKS):
        g = conv3(c * FF_CHUNK)
        val = conv3(D_FF + c * FF_CHUNK)
        act = (g * jax.nn.sigmoid(g) * val).astype(jnp.bfloat16)
        contrib = jnp.dot(act, w_down_ref[c * FF_CHUNK:(c + 1) * FF_CHUNK, :],
                          preferred_element_type=jnp.float32)
        acc_ref[...] += contrib
    out_ref[0] = _rms(acc_ref[...], fin_g_ref[...]).astype(out_ref.dtype)


def _full(shape):
    return pl.BlockSpec(shape, lambda b, t: (0,) * len(shape))


def kernel(x, meta_tokens, mix_norm_g, w_in, q_norm_g, w_uq, kv_norm_g, w_ukv, conv_w, conv_b,
           conv_ln_g, conv_ln_b, conv_out_g, attn_out_g, w_out, ffn_norm_g, w_ffn_up,
           ffn_conv_w, ffn_conv_b, w_ffn_down, final_norm_g):
    B, S, D = x.shape
    assert D == D_MODEL and S % T == 0 and mix_norm_g.shape[0] == 1
    nt = S // T + 1
    lp = nt * T
    bf16, f32 = jnp.bfloat16, jnp.float32
    row2 = lambda v: v.reshape(1, -1).astype(f32)

    w_in0 = w_in[0]
    w_in_p = jnp.concatenate([
        w_in0[:, :KR_OFF],
        jnp.zeros((D, QK_NOPE), f32), w_in0[:, KR_OFF:], jnp.zeros((D, LANES - QK_NOPE - QK_ROPE), f32),
    ], axis=1).astype(bf16)
    w_uq3 = w_uq[0].reshape(Q_LORA, N_HEADS, QK_NOPE + QK_ROPE)
    w_uqt = jnp.pad(w_uq3, ((0, 0), (0, 0), (0, HEAD_PAD - QK_NOPE - QK_ROPE))).reshape(
        Q_LORA, N_HEADS * HEAD_PAD).T.astype(bf16)
    w_ukv3 = w_ukv[0].reshape(KV_LORA, N_HEADS, QK_NOPE + V_HEAD)
    w_uk = jnp.pad(w_ukv3[:, :, :QK_NOPE], ((0, 0), (0, 0), (0, HEAD_PAD - QK_NOPE))).reshape(
        KV_LORA, N_HEADS * HEAD_PAD).astype(bf16)
    w_uvt = w_ukv3[:, :, QK_NOPE:].reshape(KV_LORA, D_ATTN).T.astype(bf16)

    pos = (jnp.arange(lp, dtype=jnp.int32) - (FRONT - N_META)).astype(f32)
    inv_freq = 1.0 / (ROPE_THETA ** (jnp.arange(0, QK_ROPE, 2, dtype=f32) / QK_ROPE))
    ang = pos[:, None] * inv_freq[None, :]
    cos, sin = jnp.cos(ang), jnp.sin(ang)
    zl = lambda n: jnp.zeros((lp, n), f32)
    kc = jnp.concatenate([zl(QK_NOPE), cos, cos, zl(LANES - QK_NOPE - QK_ROPE)], axis=1)
    ks1 = jnp.concatenate([zl(QK_NOPE + ROPE_HALF), sin, zl(LANES - QK_NOPE - QK_ROPE)], axis=1)
    ks2 = jnp.concatenate([zl(QK_NOPE), -sin, zl(LANES - QK_NOPE - ROPE_HALF)], axis=1)
    scale = (QK_NOPE + QK_ROPE) ** -0.5
    qcos = (cos * scale).T
    qsin = (sin * scale).T

    x_spec = pl.BlockSpec((1, T, D), lambda b, t: (b, jnp.maximum(t - 1, 0), 0))
    params = pltpu.CompilerParams(dimension_semantics=("arbitrary", "arbitrary"),
                                  vmem_limit_bytes=VMEM_LIMIT)

    u_n, k4, qt4, vt4 = pl.pallas_call(
        _proj_kernel,
        grid=(B, nt),
        in_specs=[
            x_spec, _full((N_META, D)), _full((1, D)), _full((D, D_IN_PAD)),
            _full((1, Q_LORA)), _full((N_HEADS * HEAD_PAD, Q_LORA)), _full((1, KV_LORA)),
            _full((KV_LORA, N_HEADS * HEAD_PAD)), _full((D_ATTN, KV_LORA)),
            _full((CONV_WIDTH, D_CONV)), _full((1, D_CONV)), _full((1, D_CONV)),
            _full((1, D_CONV)), _full((1, D_CONV)),
            pl.BlockSpec((T, LANES), lambda b, t: (t, 0)),
            pl.BlockSpec((T, LANES), lambda b, t: (t, 0)),
            pl.BlockSpec((T, LANES), lambda b, t: (t, 0)),
            pl.BlockSpec((ROPE_HALF, T), lambda b, t: (0, t)),
            pl.BlockSpec((ROPE_HALF, T), lambda b, t: (0, t)),
        ],
        out_specs=[
            pl.BlockSpec((1, T, D_CONV), lambda b, t: (b, t, 0)),
            pl.BlockSpec((1, 1, T, N_HEADS * HEAD_PAD), lambda b, t: (b, t, 0, 0)),
            pl.BlockSpec((1, 1, N_HEADS * HEAD_PAD, T), lambda b, t: (b, t, 0, 0)),
            pl.BlockSpec((1, 1, D_ATTN, T), lambda b, t: (b, t, 0, 0)),
        ],
        out_shape=[
            jax.ShapeDtypeStruct((B, lp, D_CONV), bf16),
            jax.ShapeDtypeStruct((B, nt, T, N_HEADS * HEAD_PAD), bf16),
            jax.ShapeDtypeStruct((B, nt, N_HEADS * HEAD_PAD, T), bf16),
            jax.ShapeDtypeStruct((B, nt, D_ATTN, T), bf16),
        ],
        scratch_shapes=[pltpu.VMEM((D_CONV // LANES, HALO + T, LANES), f32)],
        compiler_params=params,
        name="proj_conv_qkv",
    )(x, meta_tokens.astype(f32), row2(mix_norm_g[0]), w_in_p, row2(q_norm_g[0]), w_uqt,
      row2(kv_norm_g[0]), w_uk, w_uvt, conv_w[0].astype(f32), row2(conv_b[0]),
      row2(conv_ln_g[0]), row2(conv_ln_b[0]), row2(conv_out_g[0]), kc, ks1, ks2, qcos, qsin)

    k_meta = k4[:, 0, T - N_META:, :]
    v_meta = jnp.pad(vt4[:, 0, :, T - N_META:],
                     ((0, 0), (0, 0), (0, LANES - N_META)))

    o_n = pl.pallas_call(
        _attn_kernel,
        grid=(B, nt),
        in_specs=[
            pl.BlockSpec((1, 1, N_HEADS * HEAD_PAD, T), lambda b, i: (b, i, 0, 0)),
            pl.BlockSpec((1, nt, T, N_HEADS * HEAD_PAD), lambda b, i: (b, 0, 0, 0)),
            pl.BlockSpec((1, nt, D_ATTN, T), lambda b, i: (b, 0, 0, 0)),
            pl.BlockSpec((1, N_META, N_HEADS * HEAD_PAD), lambda b, i: (b, 0, 0)),
            pl.BlockSpec((1, D_ATTN, LANES), lambda b, i: (b, 0, 0)),
            _full((1, D_ATTN)),
        ],
        out_specs=pl.BlockSpec((1, T, D_ATTN), lambda b, i: (b, i, 0)),
        out_shape=jax.ShapeDtypeStruct((B, lp, D_ATTN), bf16),
        scratch_shapes=[pltpu.VMEM((N_HEADS, 1, T), f32), pltpu.VMEM((N_HEADS, 1, T), f32),
                        pltpu.VMEM((D_ATTN, T), f32)],
        compiler_params=params,
        name="block_causal_attn",
    )(qt4, k4, vt4, k_meta, v_meta, row2(attn_out_g[0]))

    out = pl.pallas_call(
        _ffn_kernel,
        grid=(B, nt),
        in_specs=[
            x_spec, _full((N_META, D)),
            pl.BlockSpec((1, T, D_CONV), lambda b, t: (b, t, 0)),
            pl.BlockSpec((1, T, D_ATTN), lambda b, t: (b, t, 0)),
            _full((D_CONV + D_ATTN, D)), _full((1, D)), _full((D, 2 * D_FF)),
            _full((FFN_CONV_WIDTH, 2 * D_FF)), _full((1, 2 * D_FF)), _full((D_FF, D)),
            _full((1, D)),
        ],
        out_specs=pl.BlockSpec((1, T, D), lambda b, t: (b, jnp.maximum(t - 1, 0), 0)),
        out_shape=jax.ShapeDtypeStruct((B, S, D), x.dtype),
        scratch_shapes=[pltpu.VMEM((2 * D_FF // LANES, SUBLANES + T, LANES), f32),
                        pltpu.VMEM((T, D), f32)],
        compiler_params=params,
        name="outproj_convffn",
    )(x, meta_tokens.astype(f32), u_n, o_n, w_out[0].astype(bf16), row2(ffn_norm_g[0]),
      w_ffn_up[0].astype(bf16), ffn_conv_w[0].astype(f32), row2(ffn_conv_b[0]),
      w_ffn_down[0].astype(bf16), row2(final_norm_g))
    return out
```

```python
import functools

import jax
import jax.numpy as jnp
from jax import lax
from jax.experimental import pallas as pl
from jax.experimental.pallas import tpu as pltpu

D_MODEL = 1024
CHUNK = 64
N_META = 16
D_CONV = 512
CONV_WIDTH = 31
N_HEADS = 8
QK_NOPE = 64
QK_ROPE = 32
V_HEAD = 64
D_ATTN = N_HEADS * V_HEAD
Q_LORA = 384
KV_LORA = 256
ROPE_THETA = 10000.0
D_FF = 2816
FFN_CONV_WIDTH = 3
EPS = 1e-6
NEG = -1e30

LANES = 128
SUBLANES = 8
T = 256
FRONT = T
HEAD_PAD = 128
HALO = 32
ROPE_HALF = QK_ROPE // 2
D_IN_PAD = 2 * D_CONV + Q_LORA + KV_LORA + LANES
KR_OFF = 2 * D_CONV + Q_LORA + KV_LORA
FF_CHUNK = 256
N_FF_CHUNKS = D_FF // FF_CHUNK
VMEM_LIMIT = 56 * 1024 * 1024


def _rms(x, g):
    return x * lax.rsqrt(jnp.mean(x * x, axis=-1, keepdims=True) + EPS) * g


def _tile_input(t, x_ref, meta_ref):
    h0 = jnp.concatenate(
        [jnp.zeros((T - N_META, D_MODEL), jnp.float32), meta_ref[...]], axis=0)
    return jnp.where(t == 0, h0, x_ref[0])


def _proj_kernel(x_ref, meta_ref, mix_g_ref, w_in_ref, q_g_ref, w_uqt_ref, kv_g_ref,
                 w_uk_ref, w_uvt_ref, cw_ref, cb_ref, ln_g_ref, ln_b_ref, cog_ref,
                 kc_ref, ks1_ref, ks2_ref, qcos_ref, qsin_ref,
                 u_ref, k_ref, qt_ref, vt_ref, xbuf):
    t = pl.program_id(1)
    h = _tile_input(t, x_ref, meta_ref)
    n = _rms(h, mix_g_ref[...]).astype(jnp.bfloat16)
    z = jnp.dot(n, w_in_ref[...], preferred_element_type=jnp.float32)

    a = z[:, :D_CONV]
    gate = z[:, D_CONV:2 * D_CONV]
    row = lax.broadcasted_iota(jnp.int32, (T, 1), 0) + t * T
    u = jnp.where(row >= FRONT - N_META, a * jax.nn.sigmoid(gate), 0.0)

    @pl.when(t == 0)
    def _():
        xbuf[:, 0:HALO, :] = jnp.zeros((D_CONV // LANES, HALO, LANES), jnp.float32)

    conv_slabs = []
    for s in range(D_CONV // LANES):
        xbuf[s, HALO:HALO + T, :] = u[:, s * LANES:(s + 1) * LANES]
        acc = jnp.zeros((T, LANES), jnp.float32) + cb_ref[:, s * LANES:(s + 1) * LANES]
        for k in range(CONV_WIDTH):
            off = HALO - (CONV_WIDTH - 1) + k
            acc = acc + cw_ref[k:k + 1, s * LANES:(s + 1) * LANES] * xbuf[s, off:off + T, :]
        conv_slabs.append(acc)
        xbuf[s, 0:HALO, :] = xbuf[s, T:T + HALO, :]
    c = jnp.concatenate(conv_slabs, axis=-1)
    mu = jnp.mean(c, axis=-1, keepdims=True)
    var = jnp.mean(jnp.square(c - mu), axis=-1, keepdims=True)
    c = (c - mu) * lax.rsqrt(var + EPS) * ln_g_ref[...] + ln_b_ref[...]
    c = c * jax.nn.sigmoid(c)
    u_ref[0] = _rms(c, cog_ref[...]).astype(u_ref.dtype)

    c_q = z[:, 2 * D_CONV:2 * D_CONV + Q_LORA]
    c_kv = z[:, 2 * D_CONV + Q_LORA:KR_OFF]
    k_r = z[:, KR_OFF:KR_OFF + LANES]
    qn = _rms(c_q, q_g_ref[...]).astype(jnp.bfloat16)
    kvn = _rms(c_kv, kv_g_ref[...]).astype(jnp.bfloat16)

    k_rot = (k_r * kc_ref[...]
             + pltpu.roll(k_r, ROPE_HALF, 1) * ks1_ref[...]
             + pltpu.roll(k_r, LANES - ROPE_HALF, 1) * ks2_ref[...])
    k_nope = jnp.dot(kvn, w_uk_ref[...], preferred_element_type=jnp.float32)
    for hd in range(N_HEADS):
        sl = slice(hd * HEAD_PAD, (hd + 1) * HEAD_PAD)
        k_ref[0, 0, :, sl] = (k_nope[:, sl] + k_rot).astype(k_ref.dtype)

    nt = (((1,), (1,)), ((), ()))
    vt_ref[0, 0] = lax.dot_general(w_uvt_ref[...], kvn, nt,
                                   preferred_element_type=jnp.float32).astype(vt_ref.dtype)
    qt = lax.dot_general(w_uqt_ref[...], qn, nt, preferred_element_type=jnp.float32)
    scale = (QK_NOPE + QK_ROPE) ** -0.5
    cs = qcos_ref[...]
    sn = qsin_ref[...]
    for hd in range(N_HEADS):
        b0 = hd * HEAD_PAD
        qt_ref[0, 0, b0:b0 + QK_NOPE, :] = (qt[b0:b0 + QK_NOPE] * scale).astype(qt_ref.dtype)
        x1 = qt[b0 + QK_NOPE:b0 + QK_NOPE + ROPE_HALF]
        x2 = qt[b0 + QK_NOPE + ROPE_HALF:b0 + QK_NOPE + QK_ROPE]
        qt_ref[0, 0, b0 + QK_NOPE:b0 + QK_NOPE + ROPE_HALF, :] = (
            x1 * cs - x2 * sn).astype(qt_ref.dtype)
        qt_ref[0, 0, b0 + QK_NOPE + ROPE_HALF:b0 + QK_NOPE + QK_ROPE, :] = (
            x2 * cs + x1 * sn).astype(qt_ref.dtype)
        qt_ref[0, 0, b0 + QK_NOPE + QK_ROPE:b0 + HEAD_PAD, :] = jnp.zeros(
            (HEAD_PAD - QK_NOPE - QK_ROPE, T), qt_ref.dtype)


def _attn_kernel(qt_ref, k_ref, vt_ref, km_ref, vm_ref, g_ref, o_ref, m_sc, l_sc, acc_sc):
    i = pl.program_id(1)
    heads = [(slice(hd * HEAD_PAD, (hd + 1) * HEAD_PAD), slice(hd * V_HEAD, (hd + 1) * V_HEAD))
             for hd in range(N_HEADS)]

    for hd, (qsl, vsl) in enumerate(heads):
        s = jnp.dot(km_ref[0, :, qsl], qt_ref[0, 0, qsl, :],
                    preferred_element_type=jnp.float32)
        m = jnp.max(s, axis=0, keepdims=True)
        p = jnp.exp(s - m)
        p_pad = jnp.concatenate(
            [p.astype(jnp.bfloat16), jnp.zeros((LANES - N_META, T), jnp.bfloat16)], axis=0)
        m_sc[hd] = m
        l_sc[hd] = jnp.sum(p, axis=0, keepdims=True)
        acc_sc[vsl, :] = jnp.dot(vm_ref[0, vsl, :], p_pad, preferred_element_type=jnp.float32)

    def tile_step(j, masked):
        if masked:
            key_chunk = lax.broadcasted_iota(jnp.int32, (T, T), 0) // CHUNK
            qry_chunk = lax.broadcasted_iota(jnp.int32, (T, T), 1) // CHUNK
            visible = key_chunk <= qry_chunk
        scores = [jnp.dot(k_ref[0, j, :, qsl], qt_ref[0, 0, qsl, :],
                          preferred_element_type=jnp.float32)
                  for qsl, _ in heads]
        probs, alphas = [], []
        for hd, s in enumerate(scores):
            if masked:
                s = jnp.where(visible, s, NEG)
            m = m_sc[hd]
            m_new = jnp.maximum(m, jnp.max(s, axis=0, keepdims=True))
            alpha = jnp.exp(m - m_new)
            p = jnp.exp(s - m_new)
            m_sc[hd] = m_new
            l_sc[hd] = alpha * l_sc[hd] + jnp.sum(p, axis=0, keepdims=True)
            probs.append(p.astype(jnp.bfloat16))
            alphas.append(alpha)
        for (_, vsl), p, alpha in zip(heads, probs, alphas):
            acc_sc[vsl, :] = alpha * acc_sc[vsl, :] + jnp.dot(
                vt_ref[0, j, vsl, :], p, preferred_element_type=jnp.float32)

    def full_tile(j, carry):
        tile_step(j, masked=False)
        return carry

    lax.fori_loop(1, i, full_tile, 0)

    @pl.when(i > 0)
    def _():
        tile_step(i, masked=True)

    for hd, (qsl, vsl) in enumerate(heads):
        acc_sc[vsl, :] = acc_sc[vsl, :] / l_sc[hd]
    ot = acc_sc[...]
    ot = ot * lax.rsqrt(jnp.mean(ot * ot, axis=0, keepdims=True) + EPS)
    o_ref[0] = (ot.T * g_ref[...]).astype(o_ref.dtype)


def _ffn_kernel(x_ref, meta_ref, u_ref, o_ref, w_out_ref, ffn_g_ref, w_up_ref, fcw_ref,
                fcb_ref, w_down_ref, fin_g_ref, out_ref, upbuf, acc_ref):
    t = pl.program_id(1)
    h = _tile_input(t, x_ref, meta_ref)
    h1 = (h
          + jnp.dot(u_ref[0], w_out_ref[0:D_CONV, :], preferred_element_type=jnp.float32)
          + jnp.dot(o_ref[0], w_out_ref[D_CONV:, :], preferred_element_type=jnp.float32))
    acc_ref[...] = h1
    n2 = _rms(h1, ffn_g_ref[...]).astype(jnp.bfloat16)
    row = lax.broadcasted_iota(jnp.int32, (T, 1), 0) + t * T
    valid = row >= FRONT - N_META

    @pl.when(t == 0)
    def _():
        upbuf[:, 0:SUBLANES, :] = jnp.zeros((2 * D_FF // LANES, SUBLANES, LANES), jnp.float32)

    def up_proj(c):
        return tuple(
            jnp.where(valid, jnp.dot(n2, w_up_ref[:, col0:col0 + FF_CHUNK],
                                     preferred_element_type=jnp.float32), 0.0)
            for col0 in (c * FF_CHUNK, D_FF + c * FF_CHUNK))

    def conv3(up, col0):
        outs = []
        for s in range(FF_CHUNK // LANES):
            slab = col0 // LANES + s
            cols = slice(col0 + s * LANES, col0 + (s + 1) * LANES)
            upbuf[slab, SUBLANES:SUBLANES + T, :] = up[:, s * LANES:(s + 1) * LANES]
            y = fcb_ref[:, cols] + jnp.zeros((T, LANES), jnp.float32)
            for k in range(FFN_CONV_WIDTH):
                off = SUBLANES - (FFN_CONV_WIDTH - 1) + k
                y = y + fcw_ref[k:k + 1, cols] * upbuf[slab, off:off + T, :]
            outs.append(y)
            upbuf[slab, 0:SUBLANES, :] = upbuf[slab, T:T + SUBLANES, :]
        return jnp.concatenate(outs, axis=-1)

    ups = up_proj(0)
    for c in range(N_FF_CHUNKS):
        nxt = up_proj(c + 1) if c + 1 < N_FF_CHUNKS else None
        g = conv3(ups[0], c * FF_CHUNK)
        val = conv3(ups[1], D_FF + c * FF_CHUNK)
        ups = nxt
        act = (g * jax.nn.sigmoid(g) * val).astype(jnp.bfloat16)
        contrib = jnp.dot(act, w_down_ref[c * FF_CHUNK:(c + 1) * FF_CHUNK, :],
                          preferred_element_type=jnp.float32)
        acc_ref[...] += contrib
    out_ref[0] = _rms(acc_ref[...], fin_g_ref[...]).astype(out_ref.dtype)


def _full(shape):
    return pl.BlockSpec(shape, lambda b, t: (0,) * len(shape))


def kernel(x, meta_tokens, mix_norm_g, w_in, q_norm_g, w_uq, kv_norm_g, w_ukv, conv_w, conv_b,
           conv_ln_g, conv_ln_b, conv_out_g, attn_out_g, w_out, ffn_norm_g, w_ffn_up,
           ffn_conv_w, ffn_conv_b, w_ffn_down, final_norm_g):
    B, S, D = x.shape
    assert D == D_MODEL and S % T == 0 and mix_norm_g.shape[0] == 1
    nt = S // T + 1
    lp = nt * T
    bf16, f32 = jnp.bfloat16, jnp.float32
    row2 = lambda v: v.reshape(1, -1).astype(f32)

    w_in0 = w_in[0]
    w_in_p = jnp.concatenate([
        w_in0[:, :KR_OFF],
        jnp.zeros((D, QK_NOPE), f32), w_in0[:, KR_OFF:], jnp.zeros((D, LANES - QK_NOPE - QK_ROPE), f32),
    ], axis=1).astype(bf16)
    w_uq3 = w_uq[0].reshape(Q_LORA, N_HEADS, QK_NOPE + QK_ROPE)
    w_uqt = jnp.pad(w_uq3, ((0, 0), (0, 0), (0, HEAD_PAD - QK_NOPE - QK_ROPE))).reshape(
        Q_LORA, N_HEADS * HEAD_PAD).T.astype(bf16)
    w_ukv3 = w_ukv[0].reshape(KV_LORA, N_HEADS, QK_NOPE + V_HEAD)
    w_uk = jnp.pad(w_ukv3[:, :, :QK_NOPE], ((0, 0), (0, 0), (0, HEAD_PAD - QK_NOPE))).reshape(
        KV_LORA, N_HEADS * HEAD_PAD).astype(bf16)
    w_uvt = w_ukv3[:, :, QK_NOPE:].reshape(KV_LORA, D_ATTN).T.astype(bf16)

    pos = (jnp.arange(lp, dtype=jnp.int32) - (FRONT - N_META)).astype(f32)
    inv_freq = 1.0 / (ROPE_THETA ** (jnp.arange(0, QK_ROPE, 2, dtype=f32) / QK_ROPE))
    ang = pos[:, None] * inv_freq[None, :]
    cos, sin = jnp.cos(ang), jnp.sin(ang)
    zl = lambda n: jnp.zeros((lp, n), f32)
    kc = jnp.concatenate([zl(QK_NOPE), cos, cos, zl(LANES - QK_NOPE - QK_ROPE)], axis=1)
    ks1 = jnp.concatenate([zl(QK_NOPE + ROPE_HALF), sin, zl(LANES - QK_NOPE - QK_ROPE)], axis=1)
    ks2 = jnp.concatenate([zl(QK_NOPE), -sin, zl(LANES - QK_NOPE - ROPE_HALF)], axis=1)
    scale = (QK_NOPE + QK_ROPE) ** -0.5
    qcos = (cos * scale).T
    qsin = (sin * scale).T

    x_spec = pl.BlockSpec((1, T, D), lambda b, t: (b, jnp.maximum(t - 1, 0), 0))
    params = pltpu.CompilerParams(dimension_semantics=("arbitrary", "arbitrary"),
                                  vmem_limit_bytes=VMEM_LIMIT)

    u_n, k4, qt4, vt4 = pl.pallas_call(
        _proj_kernel,
        grid=(B, nt),
        in_specs=[
            x_spec, _full((N_META, D)), _full((1, D)), _full((D, D_IN_PAD)),
            _full((1, Q_LORA)), _full((N_HEADS * HEAD_PAD, Q_LORA)), _full((1, KV_LORA)),
            _full((KV_LORA, N_HEADS * HEAD_PAD)), _full((D_ATTN, KV_LORA)),
            _full((CONV_WIDTH, D_CONV)), _full((1, D_CONV)), _full((1, D_CONV)),
            _full((1, D_CONV)), _full((1, D_CONV)),
            pl.BlockSpec((T, LANES), lambda b, t: (t, 0)),
            pl.BlockSpec((T, LANES), lambda b, t: (t, 0)),
            pl.BlockSpec((T, LANES), lambda b, t: (t, 0)),
            pl.BlockSpec((ROPE_HALF, T), lambda b, t: (0, t)),
            pl.BlockSpec((ROPE_HALF, T), lambda b, t: (0, t)),
        ],
        out_specs=[
            pl.BlockSpec((1, T, D_CONV), lambda b, t: (b, t, 0)),
            pl.BlockSpec((1, 1, T, N_HEADS * HEAD_PAD), lambda b, t: (b, t, 0, 0)),
            pl.BlockSpec((1, 1, N_HEADS * HEAD_PAD, T), lambda b, t: (b, t, 0, 0)),
            pl.BlockSpec((1, 1, D_ATTN, T), lambda b, t: (b, t, 0, 0)),
        ],
        out_shape=[
            jax.ShapeDtypeStruct((B, lp, D_CONV), bf16),
            jax.ShapeDtypeStruct((B, nt, T, N_HEADS * HEAD_PAD), bf16),
            jax.ShapeDtypeStruct((B, nt, N_HEADS * HEAD_PAD, T), bf16),
            jax.ShapeDtypeStruct((B, nt, D_ATTN, T), bf16),
        ],
        scratch_shapes=[pltpu.VMEM((D_CONV // LANES, HALO + T, LANES), f32)],
        compiler_params=params,
        name="proj_conv_qkv",
    )(x, meta_tokens.astype(f32), row2(mix_norm_g[0]), w_in_p, row2(q_norm_g[0]), w_uqt,
      row2(kv_norm_g[0]), w_uk, w_uvt, conv_w[0].astype(f32), row2(conv_b[0]),
      row2(conv_ln_g[0]), row2(conv_ln_b[0]), row2(conv_out_g[0]), kc, ks1, ks2, qcos, qsin)

    k_meta = k4[:, 0, T - N_META:, :]
    v_meta = jnp.pad(vt4[:, 0, :, T - N_META:],
                     ((0, 0), (0, 0), (0, LANES - N_META)))

    o_n = pl.pallas_call(
        _attn_kernel,
        grid=(B, nt),
        in_specs=[
            pl.BlockSpec((1, 1, N_HEADS * HEAD_PAD, T), lambda b, i: (b, i, 0, 0)),
            pl.BlockSpec((1, nt, T, N_HEADS * HEAD_PAD), lambda b, i: (b, 0, 0, 0)),
            pl.BlockSpec((1, nt, D_ATTN, T), lambda b, i: (b, 0, 0, 0)),
            pl.BlockSpec((1, N_META, N_HEADS * HEAD_PAD), lambda b, i: (b, 0, 0)),
            pl.BlockSpec((1, D_ATTN, LANES), lambda b, i: (b, 0, 0)),
            _full((1, D_ATTN)),
        ],
        out_specs=pl.BlockSpec((1, T, D_ATTN), lambda b, i: (b, i, 0)),
        out_shape=jax.ShapeDtypeStruct((B, lp, D_ATTN), bf16),
        scratch_shapes=[pltpu.VMEM((N_HEADS, 1, T), f32), pltpu.VMEM((N_HEADS, 1, T), f32),
                        pltpu.VMEM((D_ATTN, T), f32)],
        compiler_params=params,
        name="block_causal_attn",
    )(qt4, k4, vt4, k_meta, v_meta, row2(attn_out_g[0]))

    out = pl.pallas_call(
        _ffn_kernel,
        grid=(B, nt),
        in_specs=[
            x_spec, _full((N_META, D)),
            pl.BlockSpec((1, T, D_CONV), lambda b, t: (b, t, 0)),
            pl.BlockSpec((1, T, D_ATTN), lambda b, t: (b, t, 0)),
            _full((D_CONV + D_ATTN, D)), _full((1, D)), _full((D, 2 * D_FF)),
            _full((FFN_CONV_WIDTH, 2 * D_FF)), _full((1, 2 * D_FF)), _full((D_FF, D)),
            _full((1, D)),
        ],
        out_specs=pl.BlockSpec((1, T, D), lambda b, t: (b, jnp.maximum(t - 1, 0), 0)),
        out_shape=jax.ShapeDtypeStruct((B, S, D), x.dtype),
        scratch_shapes=[pltpu.VMEM((2 * D_FF // LANES, SUBLANES + T, LANES), f32),
                        pltpu.VMEM((T, D), f32)],
        compiler_params=params,
        name="outproj_convffn",
    )(x, meta_tokens.astype(f32), u_n, o_n, w_out[0].astype(bf16), row2(ffn_norm_g[0]),
      w_ffn_up[0].astype(bf16), ffn_conv_w[0].astype(f32), row2(ffn_conv_b[0]),
      w_ffn_down[0].astype(bf16), row2(final_norm_g))
    return out
```

```python
import functools

import jax
import jax.numpy as jnp
from jax import lax
from jax.experimental import pallas as pl
from jax.experimental.pallas import tpu as pltpu

D_MODEL = 1024
CHUNK = 64
N_META = 16
D_CONV = 512
CONV_WIDTH = 31
N_HEADS = 8
QK_NOPE = 64
QK_ROPE = 32
V_HEAD = 64
D_ATTN = N_HEADS * V_HEAD
Q_LORA = 384
KV_LORA = 256
ROPE_THETA = 10000.0
D_FF = 2816
FFN_CONV_WIDTH = 3
EPS = 1e-6
NEG = -1e30

LANES = 128
SUBLANES = 8
T = 256
FRONT = T
HEAD_PAD = 128
HALO = 32
ROPE_HALF = QK_ROPE // 2
D_IN_PAD = 2 * D_CONV + Q_LORA + KV_LORA + LANES
KR_OFF = 2 * D_CONV + Q_LORA + KV_LORA
FF_CHUNK = 256
N_FF_CHUNKS = D_FF // FF_CHUNK
VMEM_LIMIT = 56 * 1024 * 1024
Q_SCALE = (QK_NOPE + QK_ROPE) ** -0.5 * 1.4426950408889634


def _rms(x, g):
    return x * lax.rsqrt(jnp.mean(x * x, axis=-1, keepdims=True) + EPS) * g


def _tile_input(t, x_ref, meta_ref):
    h0 = jnp.concatenate(
        [jnp.zeros((T - N_META, D_MODEL), jnp.float32), meta_ref[...]], axis=0)
    return jnp.where(t == 0, h0, x_ref[0])


def _proj_kernel(x_ref, meta_ref, mix_g_ref, w_in_ref, q_g_ref, w_uqt_ref, kv_g_ref,
                 w_uk_ref, w_uvt_ref, cw_ref, cb_ref, ln_g_ref, ln_b_ref, cog_ref,
                 kc_ref, ks1_ref, ks2_ref, qcos_ref, qsin_ref,
                 u_ref, k_ref, qt_ref, vt_ref, xbuf):
    t = pl.program_id(1)
    h = _tile_input(t, x_ref, meta_ref)
    n = _rms(h, mix_g_ref[...]).astype(jnp.bfloat16)
    z = jnp.dot(n, w_in_ref[...], preferred_element_type=jnp.float32)

    a = z[:, :D_CONV]
    gate = z[:, D_CONV:2 * D_CONV]
    row = lax.broadcasted_iota(jnp.int32, (T, 1), 0) + t * T
    u = jnp.where(row >= FRONT - N_META, a * jax.nn.sigmoid(gate), 0.0)

    @pl.when(t == 0)
    def _():
        xbuf[:, 0:HALO, :] = jnp.zeros((D_CONV // LANES, HALO, LANES), jnp.float32)

    conv_slabs = []
    for s in range(D_CONV // LANES):
        xbuf[s, HALO:HALO + T, :] = u[:, s * LANES:(s + 1) * LANES]
        acc = jnp.zeros((T, LANES), jnp.float32) + cb_ref[:, s * LANES:(s + 1) * LANES]
        for k in range(CONV_WIDTH):
            off = HALO - (CONV_WIDTH - 1) + k
            acc = acc + cw_ref[k:k + 1, s * LANES:(s + 1) * LANES] * xbuf[s, off:off + T, :]
        conv_slabs.append(acc)
        xbuf[s, 0:HALO, :] = xbuf[s, T:T + HALO, :]
    c = jnp.concatenate(conv_slabs, axis=-1)
    mu = jnp.mean(c, axis=-1, keepdims=True)
    var = jnp.mean(jnp.square(c - mu), axis=-1, keepdims=True)
    c = (c - mu) * lax.rsqrt(var + EPS) * ln_g_ref[...] + ln_b_ref[...]
    c = c * jax.nn.sigmoid(c)
    u_ref[0] = _rms(c, cog_ref[...]).astype(u_ref.dtype)

    c_q = z[:, 2 * D_CONV:2 * D_CONV + Q_LORA]
    c_kv = z[:, 2 * D_CONV + Q_LORA:KR_OFF]
    k_r = z[:, KR_OFF:KR_OFF + LANES]
    qn = _rms(c_q, q_g_ref[...]).astype(jnp.bfloat16)
    kvn = _rms(c_kv, kv_g_ref[...]).astype(jnp.bfloat16)

    k_rot = (k_r * kc_ref[...]
             + pltpu.roll(k_r, ROPE_HALF, 1) * ks1_ref[...]
             + pltpu.roll(k_r, LANES - ROPE_HALF, 1) * ks2_ref[...])
    k_nope = jnp.dot(kvn, w_uk_ref[...], preferred_element_type=jnp.float32)
    for hd in range(N_HEADS):
        sl = slice(hd * HEAD_PAD, (hd + 1) * HEAD_PAD)
        k_ref[0, 0, :, sl] = (k_nope[:, sl] + k_rot).astype(k_ref.dtype)

    nt = (((1,), (1,)), ((), ()))
    vt_ref[0, 0] = lax.dot_general(w_uvt_ref[...], kvn, nt,
                                   preferred_element_type=jnp.float32).astype(vt_ref.dtype)
    qt = lax.dot_general(w_uqt_ref[...], qn, nt, preferred_element_type=jnp.float32)
    scale = Q_SCALE
    cs = qcos_ref[...]
    sn = qsin_ref[...]
    for hd in range(N_HEADS):
        b0 = hd * HEAD_PAD
        qt_ref[0, 0, b0:b0 + QK_NOPE, :] = (qt[b0:b0 + QK_NOPE] * scale).astype(qt_ref.dtype)
        x1 = qt[b0 + QK_NOPE:b0 + QK_NOPE + ROPE_HALF]
        x2 = qt[b0 + QK_NOPE + ROPE_HALF:b0 + QK_NOPE + QK_ROPE]
        qt_ref[0, 0, b0 + QK_NOPE:b0 + QK_NOPE + ROPE_HALF, :] = (
            x1 * cs - x2 * sn).astype(qt_ref.dtype)
        qt_ref[0, 0, b0 + QK_NOPE + ROPE_HALF:b0 + QK_NOPE + QK_ROPE, :] = (
            x2 * cs + x1 * sn).astype(qt_ref.dtype)
        qt_ref[0, 0, b0 + QK_NOPE + QK_ROPE:b0 + HEAD_PAD, :] = jnp.zeros(
            (HEAD_PAD - QK_NOPE - QK_ROPE, T), qt_ref.dtype)


def _attn_kernel(qt_ref, k_ref, vt_ref, km_ref, vm_ref, g_ref, o_ref,
                 m_sc, l_sc, acc_sc, alpha_sc, s_sc, p_sc):
    i = pl.program_id(1)
    heads = [(slice(hd * HEAD_PAD, (hd + 1) * HEAD_PAD), slice(hd * V_HEAD, (hd + 1) * V_HEAD))
             for hd in range(N_HEADS)]

    for hd, (qsl, vsl) in enumerate(heads):
        s = jnp.dot(km_ref[0, :, qsl], qt_ref[0, 0, qsl, :],
                    preferred_element_type=jnp.float32)
        m = jnp.max(s, axis=0, keepdims=True)
        p = jnp.exp2(s - m)
        p_pad = jnp.concatenate(
            [p.astype(jnp.bfloat16), jnp.zeros((LANES - N_META, T), jnp.bfloat16)], axis=0)
        m_sc[hd] = m
        l_sc[hd] = jnp.sum(p, axis=0, keepdims=True)
        acc_sc[vsl, :] = jnp.dot(vm_ref[0, vsl, :], p_pad, preferred_element_type=jnp.float32)

    def scores(j, slot):
        for hd, (qsl, _) in enumerate(heads):
            s_sc[slot, hd] = jnp.dot(k_ref[0, j, :, qsl], qt_ref[0, 0, qsl, :],
                                     preferred_element_type=jnp.float32)

    def softmax(slot, masked):
        if masked:
            key_chunk = lax.broadcasted_iota(jnp.int32, (T, T), 0) // CHUNK
            qry_chunk = lax.broadcasted_iota(jnp.int32, (T, T), 1) // CHUNK
            visible = key_chunk <= qry_chunk
        for hd in range(N_HEADS):
            s = s_sc[slot, hd]
            if masked:
                s = jnp.where(visible, s, NEG)
            m = m_sc[hd]
            m_new = jnp.maximum(m, jnp.max(s, axis=0, keepdims=True))
            alpha = jnp.exp2(m - m_new)
            p = jnp.exp2(s - m_new)
            m_sc[hd] = m_new
            l_sc[hd] = alpha * l_sc[hd] + jnp.sum(p, axis=0, keepdims=True)
            alpha_sc[slot, hd] = alpha
            p_sc[slot, hd] = p.astype(jnp.bfloat16)

    def values(j, slot):
        for hd, (_, vsl) in enumerate(heads):
            acc_sc[vsl, :] = alpha_sc[slot, hd] * acc_sc[vsl, :] + jnp.dot(
                vt_ref[0, j, vsl, :], p_sc[slot, hd], preferred_element_type=jnp.float32)

    def stage(n, slot, masked, prefetch):
        if prefetch:
            scores(n + 1, 1 - slot)
        values(n - 1, 1 - slot)
        softmax(slot, masked)

    @pl.when(i > 0)
    def _():
        p_sc[0] = jnp.zeros((N_HEADS, T, T), jnp.bfloat16)
        alpha_sc[0] = jnp.ones((N_HEADS, 1, T), jnp.float32)
        scores(1, 1)
        n_pairs = (i - 1) // 2

        def pair(kk, carry):
            n = 2 * kk + 1
            stage(n, 1, masked=False, prefetch=True)
            stage(n + 1, 0, masked=False, prefetch=True)
            return carry

        lax.fori_loop(0, n_pairs, pair, 0)
        n0 = 2 * n_pairs + 1

        @pl.when(n0 == i)
        def _():
            stage(n0, 1, masked=True, prefetch=False)
            values(n0, 1)

        @pl.when(n0 != i)
        def _():
            stage(n0, 1, masked=False, prefetch=True)
            stage(n0 + 1, 0, masked=True, prefetch=False)
            values(n0 + 1, 0)

    for hd, (qsl, vsl) in enumerate(heads):
        acc_sc[vsl, :] = acc_sc[vsl, :] / l_sc[hd]
    ot = acc_sc[...]
    ot = ot * lax.rsqrt(jnp.mean(ot * ot, axis=0, keepdims=True) + EPS)
    o_ref[0] = (ot.T * g_ref[...]).astype(o_ref.dtype)


def _ffn_kernel(x_ref, meta_ref, u_ref, o_ref, w_out_ref, ffn_g_ref, w_up_ref, fcw_ref,
                fcb_ref, w_down_ref, fin_g_ref, out_ref, upbuf, acc_ref):
    t = pl.program_id(1)
    h = _tile_input(t, x_ref, meta_ref)
    h1 = (h
          + jnp.dot(u_ref[0], w_out_ref[0:D_CONV, :], preferred_element_type=jnp.float32)
          + jnp.dot(o_ref[0], w_out_ref[D_CONV:, :], preferred_element_type=jnp.float32))
    acc_ref[...] = h1
    n2 = _rms(h1, ffn_g_ref[...]).astype(jnp.bfloat16)
    row = lax.broadcasted_iota(jnp.int32, (T, 1), 0) + t * T
    valid = row >= FRONT - N_META

    @pl.when(t == 0)
    def _():
        upbuf[:, 0:SUBLANES, :] = jnp.zeros((2 * D_FF // LANES, SUBLANES, LANES), jnp.float32)

    def up_proj(c):
        return tuple(
            jnp.where(valid, jnp.dot(n2, w_up_ref[:, col0:col0 + FF_CHUNK],
                                     preferred_element_type=jnp.float32), 0.0)
            for col0 in (c * FF_CHUNK, D_FF + c * FF_CHUNK))

    def conv3(up, col0):
        outs = []
        for s in range(FF_CHUNK // LANES):
            slab = col0 // LANES + s
            cols = slice(col0 + s * LANES, col0 + (s + 1) * LANES)
            upbuf[slab, SUBLANES:SUBLANES + T, :] = up[:, s * LANES:(s + 1) * LANES]
            y = fcb_ref[:, cols] + jnp.zeros((T, LANES), jnp.float32)
            for k in range(FFN_CONV_WIDTH):
                off = SUBLANES - (FFN_CONV_WIDTH - 1) + k
                y = y + fcw_ref[k:k + 1, cols] * upbuf[slab, off:off + T, :]
            outs.append(y)
            upbuf[slab, 0:SUBLANES, :] = upbuf[slab, T:T + SUBLANES, :]
        return jnp.concatenate(outs, axis=-1)

    ups = up_proj(0)
    for c in range(N_FF_CHUNKS):
        nxt = up_proj(c + 1) if c + 1 < N_FF_CHUNKS else None
        g = conv3(ups[0], c * FF_CHUNK)
        val = conv3(ups[1], D_FF + c * FF_CHUNK)
        ups = nxt
        act = (g * jax.nn.sigmoid(g) * val).astype(jnp.bfloat16)
        contrib = jnp.dot(act, w_down_ref[c * FF_CHUNK:(c + 1) * FF_CHUNK, :],
                          preferred_element_type=jnp.float32)
        acc_ref[...] += contrib
    out_ref[0] = _rms(acc_ref[...], fin_g_ref[...]).astype(out_ref.dtype)


def _full(shape):
    return pl.BlockSpec(shape, lambda b, t: (0,) * len(shape))


def kernel(x, meta_tokens, mix_norm_g, w_in, q_norm_g, w_uq, kv_norm_g, w_ukv, conv_w, conv_b,
           conv_ln_g, conv_ln_b, conv_out_g, attn_out_g, w_out, ffn_norm_g, w_ffn_up,
           ffn_conv_w, ffn_conv_b, w_ffn_down, final_norm_g):
    B, S, D = x.shape
    assert D == D_MODEL and S % T == 0 and mix_norm_g.shape[0] == 1
    nt = S // T + 1
    lp = nt * T
    bf16, f32 = jnp.bfloat16, jnp.float32
    row2 = lambda v: v.reshape(1, -1).astype(f32)

    w_in0 = w_in[0]
    w_in_p = jnp.concatenate([
        w_in0[:, :KR_OFF],
        jnp.zeros((D, QK_NOPE), f32), w_in0[:, KR_OFF:], jnp.zeros((D, LANES - QK_NOPE - QK_ROPE), f32),
    ], axis=1).astype(bf16)
    w_uq3 = w_uq[0].reshape(Q_LORA, N_HEADS, QK_NOPE + QK_ROPE)
    w_uqt = jnp.pad(w_uq3, ((0, 0), (0, 0), (0, HEAD_PAD - QK_NOPE - QK_ROPE))).reshape(
        Q_LORA, N_HEADS * HEAD_PAD).T.astype(bf16)
    w_ukv3 = w_ukv[0].reshape(KV_LORA, N_HEADS, QK_NOPE + V_HEAD)
    w_uk = jnp.pad(w_ukv3[:, :, :QK_NOPE], ((0, 0), (0, 0), (0, HEAD_PAD - QK_NOPE))).reshape(
        KV_LORA, N_HEADS * HEAD_PAD).astype(bf16)
    w_uvt = w_ukv3[:, :, QK_NOPE:].reshape(KV_LORA, D_ATTN).T.astype(bf16)

    pos = (jnp.arange(lp, dtype=jnp.int32) - (FRONT - N_META)).astype(f32)
    inv_freq = 1.0 / (ROPE_THETA ** (jnp.arange(0, QK_ROPE, 2, dtype=f32) / QK_ROPE))
    ang = pos[:, None] * inv_freq[None, :]
    cos, sin = jnp.cos(ang), jnp.sin(ang)
    zl = lambda n: jnp.zeros((lp, n), f32)
    kc = jnp.concatenate([zl(QK_NOPE), cos, cos, zl(LANES - QK_NOPE - QK_ROPE)], axis=1)
    ks1 = jnp.concatenate([zl(QK_NOPE + ROPE_HALF), sin, zl(LANES - QK_NOPE - QK_ROPE)], axis=1)
    ks2 = jnp.concatenate([zl(QK_NOPE), -sin, zl(LANES - QK_NOPE - ROPE_HALF)], axis=1)
    qcos = (cos * Q_SCALE).T
    qsin = (sin * Q_SCALE).T

    x_spec = pl.BlockSpec((1, T, D), lambda b, t: (b, jnp.maximum(t - 1, 0), 0))
    params = pltpu.CompilerParams(dimension_semantics=("arbitrary", "arbitrary"),
                                  vmem_limit_bytes=VMEM_LIMIT)

    u_n, k4, qt4, vt4 = pl.pallas_call(
        _proj_kernel,
        grid=(B, nt),
        in_specs=[
            x_spec, _full((N_META, D)), _full((1, D)), _full((D, D_IN_PAD)),
            _full((1, Q_LORA)), _full((N_HEADS * HEAD_PAD, Q_LORA)), _full((1, KV_LORA)),
            _full((KV_LORA, N_HEADS * HEAD_PAD)), _full((D_ATTN, KV_LORA)),
            _full((CONV_WIDTH, D_CONV)), _full((1, D_CONV)), _full((1, D_CONV)),
            _full((1, D_CONV)), _full((1, D_CONV)),
            pl.BlockSpec((T, LANES), lambda b, t: (t, 0)),
            pl.BlockSpec((T, LANES), lambda b, t: (t, 0)),
            pl.BlockSpec((T, LANES), lambda b, t: (t, 0)),
            pl.BlockSpec((ROPE_HALF, T), lambda b, t: (0, t)),
            pl.BlockSpec((ROPE_HALF, T), lambda b, t: (0, t)),
        ],
        out_specs=[
            pl.BlockSpec((1, T, D_CONV), lambda b, t: (b, t, 0)),
            pl.BlockSpec((1, 1, T, N_HEADS * HEAD_PAD), lambda b, t: (b, t, 0, 0)),
            pl.BlockSpec((1, 1, N_HEADS * HEAD_PAD, T), lambda b, t: (b, t, 0, 0)),
            pl.BlockSpec((1, 1, D_ATTN, T), lambda b, t: (b, t, 0, 0)),
        ],
        out_shape=[
            jax.ShapeDtypeStruct((B, lp, D_CONV), bf16),
            jax.ShapeDtypeStruct((B, nt, T, N_HEADS * HEAD_PAD), bf16),
            jax.ShapeDtypeStruct((B, nt, N_HEADS * HEAD_PAD, T), bf16),
            jax.ShapeDtypeStruct((B, nt, D_ATTN, T), bf16),
        ],
        scratch_shapes=[pltpu.VMEM((D_CONV // LANES, HALO + T, LANES), f32)],
        compiler_params=params,
        name="proj_conv_qkv",
    )(x, meta_tokens.astype(f32), row2(mix_norm_g[0]), w_in_p, row2(q_norm_g[0]), w_uqt,
      row2(kv_norm_g[0]), w_uk, w_uvt, conv_w[0].astype(f32), row2(conv_b[0]),
      row2(conv_ln_g[0]), row2(conv_ln_b[0]), row2(conv_out_g[0]), kc, ks1, ks2, qcos, qsin)

    k_meta = k4[:, 0, T - N_META:, :]
    v_meta = jnp.pad(vt4[:, 0, :, T - N_META:],
                     ((0, 0), (0, 0), (0, LANES - N_META)))

    o_n = pl.pallas_call(
        _attn_kernel,
        grid=(B, nt),
        in_specs=[
            pl.BlockSpec((1, 1, N_HEADS * HEAD_PAD, T), lambda b, i: (b, i, 0, 0)),
            pl.BlockSpec((1, nt, T, N_HEADS * HEAD_PAD), lambda b, i: (b, 0, 0, 0)),
            pl.BlockSpec((1, nt, D_ATTN, T), lambda b, i: (b, 0, 0, 0)),
            pl.BlockSpec((1, N_META, N_HEADS * HEAD_PAD), lambda b, i: (b, 0, 0)),
            pl.BlockSpec((1, D_ATTN, LANES), lambda b, i: (b, 0, 0)),
            _full((1, D_ATTN)),
        ],
        out_specs=pl.BlockSpec((1, T, D_ATTN), lambda b, i: (b, i, 0)),
        out_shape=jax.ShapeDtypeStruct((B, lp, D_ATTN), bf16),
        scratch_shapes=[pltpu.VMEM((N_HEADS, 1, T), f32),
                        pltpu.VMEM((N_HEADS, 1, T), f32),
                        pltpu.VMEM((D_ATTN, T), f32),
                        pltpu.VMEM((2, N_HEADS, 1, T), f32),
                        pltpu.VMEM((2, N_HEADS, T, T), f32),
                        pltpu.VMEM((2, N_HEADS, T, T), bf16)],
        compiler_params=params,
        name="block_causal_attn",
    )(qt4, k4, vt4, k_meta, v_meta, row2(attn_out_g[0]))

    out = pl.pallas_call(
        _ffn_kernel,
        grid=(B, nt),
        in_specs=[
            x_spec, _full((N_META, D)),
            pl.BlockSpec((1, T, D_CONV), lambda b, t: (b, t, 0)),
            pl.BlockSpec((1, T, D_ATTN), lambda b, t: (b, t, 0)),
            _full((D_CONV + D_ATTN, D)), _full((1, D)), _full((D, 2 * D_FF)),
            _full((FFN_CONV_WIDTH, 2 * D_FF)), _full((1, 2 * D_FF)), _full((D_FF, D)),
            _full((1, D)),
        ],
        out_specs=pl.BlockSpec((1, T, D), lambda b, t: (b, jnp.maximum(t - 1, 0), 0)),
        out_shape=jax.ShapeDtypeStruct((B, S, D), x.dtype),
        scratch_shapes=[pltpu.VMEM((2 * D_FF // LANES, SUBLANES + T, LANES), f32),
                        pltpu.VMEM((T, D), f32)],
        compiler_params=params,
        name="outproj_convffn",
    )(x, meta_tokens.astype(f32), u_n, o_n, w_out[0].astype(bf16), row2(ffn_norm_g[0]),
      w_ffn_up[0].astype(bf16), ffn_conv_w[0].astype(f32), row2(ffn_conv_b[0]),
      w_ffn_down[0].astype(bf16), row2(final_norm_g))
    return out
```

```python
import functools

import jax
import jax.numpy as jnp
from jax import lax
from jax.experimental import pallas as pl
from jax.experimental.pallas import tpu as pltpu

D_MODEL = 1024
CHUNK = 64
N_META = 16
D_CONV = 512
CONV_WIDTH = 31
N_HEADS = 8
QK_NOPE = 64
QK_ROPE = 32
V_HEAD = 64
D_ATTN = N_HEADS * V_HEAD
Q_LORA = 384
KV_LORA = 256
ROPE_THETA = 10000.0
D_FF = 2816
FFN_CONV_WIDTH = 3
EPS = 1e-6
NEG = -1e30

LANES = 128
SUBLANES = 8
T = 256
FRONT = T
HEAD_PAD = 128
HALO = 32
ROPE_HALF = QK_ROPE // 2
D_IN_PAD = 2 * D_CONV + Q_LORA + KV_LORA + LANES
KR_OFF = 2 * D_CONV + Q_LORA + KV_LORA
FF_CHUNK = 256
N_FF_CHUNKS = D_FF // FF_CHUNK
VMEM_LIMIT = 56 * 1024 * 1024
Q_SCALE = (QK_NOPE + QK_ROPE) ** -0.5 * 1.4426950408889634


def _rms(x, g):
    return x * lax.rsqrt(jnp.mean(x * x, axis=-1, keepdims=True) + EPS) * g


def _tile_input(t, x_ref, meta_ref):
    h0 = jnp.concatenate(
        [jnp.zeros((T - N_META, D_MODEL), jnp.float32), meta_ref[...]], axis=0)
    return jnp.where(t == 0, h0, x_ref[0])


def _proj_kernel(x_ref, meta_ref, mix_g_ref, w_in_ref, q_g_ref, w_uqt_ref, kv_g_ref,
                 w_uk_ref, w_uvt_ref, cw_ref, cb_ref, ln_g_ref, ln_b_ref, cog_ref,
                 kc_ref, ks1_ref, ks2_ref, qcos_ref, qsin_ref,
                 u_ref, k_ref, qt_ref, vt_ref, xbuf):
    t = pl.program_id(1)
    h = _tile_input(t, x_ref, meta_ref)
    n = _rms(h, mix_g_ref[...]).astype(jnp.bfloat16)
    z = jnp.dot(n, w_in_ref[...], preferred_element_type=jnp.float32)

    a = z[:, :D_CONV]
    gate = z[:, D_CONV:2 * D_CONV]
    row = lax.broadcasted_iota(jnp.int32, (T, 1), 0) + t * T
    u = jnp.where(row >= FRONT - N_META, a * jax.nn.sigmoid(gate), 0.0)

    @pl.when(t == 0)
    def _():
        xbuf[:, 0:HALO, :] = jnp.zeros((D_CONV // LANES, HALO, LANES), jnp.float32)

    conv_slabs = []
    for s in range(D_CONV // LANES):
        xbuf[s, HALO:HALO + T, :] = u[:, s * LANES:(s + 1) * LANES]
        acc = jnp.zeros((T, LANES), jnp.float32) + cb_ref[:, s * LANES:(s + 1) * LANES]
        for k in range(CONV_WIDTH):
            off = HALO - (CONV_WIDTH - 1) + k
            acc = acc + cw_ref[k:k + 1, s * LANES:(s + 1) * LANES] * xbuf[s, off:off + T, :]
        conv_slabs.append(acc)
        xbuf[s, 0:HALO, :] = xbuf[s, T:T + HALO, :]
    c = jnp.concatenate(conv_slabs, axis=-1)
    mu = jnp.mean(c, axis=-1, keepdims=True)
    var = jnp.mean(jnp.square(c - mu), axis=-1, keepdims=True)
    c = (c - mu) * lax.rsqrt(var + EPS) * ln_g_ref[...] + ln_b_ref[...]
    c = c * jax.nn.sigmoid(c)
    u_ref[0] = _rms(c, cog_ref[...]).astype(u_ref.dtype)

    c_q = z[:, 2 * D_CONV:2 * D_CONV + Q_LORA]
    c_kv = z[:, 2 * D_CONV + Q_LORA:KR_OFF]
    k_r = z[:, KR_OFF:KR_OFF + LANES]
    qn = _rms(c_q, q_g_ref[...]).astype(jnp.bfloat16)
    kvn = _rms(c_kv, kv_g_ref[...]).astype(jnp.bfloat16)

    k_rot = (k_r * kc_ref[...]
             + pltpu.roll(k_r, ROPE_HALF, 1) * ks1_ref[...]
             + pltpu.roll(k_r, LANES - ROPE_HALF, 1) * ks2_ref[...])
    k_nope = jnp.dot(kvn, w_uk_ref[...], preferred_element_type=jnp.float32)
    for hd in range(N_HEADS):
        sl = slice(hd * HEAD_PAD, (hd + 1) * HEAD_PAD)
        k_ref[0, 0, :, sl] = (k_nope[:, sl] + k_rot).astype(k_ref.dtype)

    nt = (((1,), (1,)), ((), ()))
    vt_ref[0, 0] = lax.dot_general(w_uvt_ref[...], kvn, nt,
                                   preferred_element_type=jnp.float32).astype(vt_ref.dtype)
    qt = lax.dot_general(w_uqt_ref[...], qn, nt, preferred_element_type=jnp.float32)
    scale = Q_SCALE
    cs = qcos_ref[...]
    sn = qsin_ref[...]
    for hd in range(N_HEADS):
        b0 = hd * HEAD_PAD
        qt_ref[0, 0, b0:b0 + QK_NOPE, :] = (qt[b0:b0 + QK_NOPE] * scale).astype(qt_ref.dtype)
        x1 = qt[b0 + QK_NOPE:b0 + QK_NOPE + ROPE_HALF]
        x2 = qt[b0 + QK_NOPE + ROPE_HALF:b0 + QK_NOPE + QK_ROPE]
        qt_ref[0, 0, b0 + QK_NOPE:b0 + QK_NOPE + ROPE_HALF, :] = (
            x1 * cs - x2 * sn).astype(qt_ref.dtype)
        qt_ref[0, 0, b0 + QK_NOPE + ROPE_HALF:b0 + QK_NOPE + QK_ROPE, :] = (
            x2 * cs + x1 * sn).astype(qt_ref.dtype)
        qt_ref[0, 0, b0 + QK_NOPE + QK_ROPE:b0 + HEAD_PAD, :] = jnp.zeros(
            (HEAD_PAD - QK_NOPE - QK_ROPE, T), qt_ref.dtype)


def _attn_kernel(qt_ref, k_ref, vt_ref, km_ref, vm_ref, g_ref, o_ref,
                 m_sc, l_sc, acc_sc, alpha_sc, smax_sc, s_sc, p_sc):
    i = pl.program_id(1)
    heads = [(slice(hd * HEAD_PAD, (hd + 1) * HEAD_PAD), slice(hd * V_HEAD, (hd + 1) * V_HEAD))
             for hd in range(N_HEADS)]

    meta_s = [jnp.dot(km_ref[0, :, qsl], qt_ref[0, 0, qsl, :],
                      preferred_element_type=jnp.float32) for qsl, _ in heads]
    meta_p = []
    for hd, s in enumerate(meta_s):
        m = jnp.max(s, axis=0, keepdims=True)
        p = jnp.exp2(s - m)
        m_sc[hd] = m
        l_sc[hd] = jnp.sum(p, axis=0, keepdims=True)
        meta_p.append(jnp.concatenate(
            [p.astype(jnp.bfloat16), jnp.zeros((LANES - N_META, T), jnp.bfloat16)], axis=0))
    for (_, vsl), p_pad in zip(heads, meta_p):
        acc_sc[vsl, :] = jnp.dot(vm_ref[0, vsl, :], p_pad, preferred_element_type=jnp.float32)

    def scores(j, slot):
        for hd, (qsl, _) in enumerate(heads):
            s = jnp.dot(k_ref[0, j, :, qsl], qt_ref[0, 0, qsl, :],
                        preferred_element_type=jnp.float32)
            s_sc[slot, hd] = s
            smax_sc[slot, hd] = jnp.max(s, axis=0, keepdims=True)

    def softmax(slot, masked):
        if masked:
            key_chunk = lax.broadcasted_iota(jnp.int32, (T, T), 0) // CHUNK
            qry_chunk = lax.broadcasted_iota(jnp.int32, (T, T), 1) // CHUNK
            visible = key_chunk <= qry_chunk
        for hd in range(N_HEADS):
            s = s_sc[slot, hd]
            if masked:
                s = jnp.where(visible, s, NEG)
                tile_max = jnp.max(s, axis=0, keepdims=True)
            else:
                tile_max = smax_sc[slot, hd]
            m = m_sc[hd]
            m_new = jnp.maximum(m, tile_max)
            alpha = jnp.exp2(m - m_new)
            p = jnp.exp2(s - m_new)
            m_sc[hd] = m_new
            l_sc[hd] = alpha * l_sc[hd] + jnp.sum(p, axis=0, keepdims=True)
            alpha_sc[slot, hd] = alpha
            p_sc[slot, hd] = p.astype(jnp.bfloat16)

    def values(j, slot):
        for hd, (_, vsl) in enumerate(heads):
            acc_sc[vsl, :] = alpha_sc[slot, hd] * acc_sc[vsl, :] + jnp.dot(
                vt_ref[0, j, vsl, :], p_sc[slot, hd], preferred_element_type=jnp.float32)

    def stage(n, slot, masked, prefetch):
        if prefetch:
            scores(n + 1, 1 - slot)
        values(n - 1, 1 - slot)
        softmax(slot, masked)

    @pl.when(i > 0)
    def _():
        p_sc[0] = jnp.zeros((N_HEADS, T, T), jnp.bfloat16)
        alpha_sc[0] = jnp.ones((N_HEADS, 1, T), jnp.float32)
        scores(1, 1)
        n_pairs = (i - 1) // 2

        def pair(kk, carry):
            n = 2 * kk + 1
            stage(n, 1, masked=False, prefetch=True)
            stage(n + 1, 0, masked=False, prefetch=True)
            return carry

        lax.fori_loop(0, n_pairs, pair, 0)
        n0 = 2 * n_pairs + 1

        @pl.when(n0 == i)
        def _():
            stage(n0, 1, masked=True, prefetch=False)
            values(n0, 1)

        @pl.when(n0 != i)
        def _():
            stage(n0, 1, masked=False, prefetch=True)
            stage(n0 + 1, 0, masked=True, prefetch=False)
            values(n0 + 1, 0)

    for hd, (qsl, vsl) in enumerate(heads):
        acc_sc[vsl, :] = acc_sc[vsl, :] / l_sc[hd]
    ot = acc_sc[...]
    ot = ot * lax.rsqrt(jnp.mean(ot * ot, axis=0, keepdims=True) + EPS)
    o_ref[0] = (ot.T * g_ref[...]).astype(o_ref.dtype)


def _ffn_kernel(x_ref, meta_ref, u_ref, o_ref, w_out_ref, ffn_g_ref, w_up_ref, fcw_ref,
                fcb_ref, w_down_ref, fin_g_ref, out_ref, upbuf, acc_ref):
    t = pl.program_id(1)
    h = _tile_input(t, x_ref, meta_ref)
    h1 = (h
          + jnp.dot(u_ref[0], w_out_ref[0:D_CONV, :], preferred_element_type=jnp.float32)
          + jnp.dot(o_ref[0], w_out_ref[D_CONV:, :], preferred_element_type=jnp.float32))
    acc_ref[...] = h1
    n2 = _rms(h1, ffn_g_ref[...]).astype(jnp.bfloat16)
    row = lax.broadcasted_iota(jnp.int32, (T, 1), 0) + t * T
    valid = row >= FRONT - N_META

    @pl.when(t == 0)
    def _():
        upbuf[:, 0:SUBLANES, :] = jnp.zeros((2 * D_FF // LANES, SUBLANES, LANES), jnp.float32)

    def up_proj(c):
        return tuple(
            jnp.where(valid, jnp.dot(n2, w_up_ref[:, col0:col0 + FF_CHUNK],
                                     preferred_element_type=jnp.float32), 0.0)
            for col0 in (c * FF_CHUNK, D_FF + c * FF_CHUNK))

    def conv3(up, col0):
        outs = []
        for s in range(FF_CHUNK // LANES):
            slab = col0 // LANES + s
            cols = slice(col0 + s * LANES, col0 + (s + 1) * LANES)
            upbuf[slab, SUBLANES:SUBLANES + T, :] = up[:, s * LANES:(s + 1) * LANES]
            y = fcb_ref[:, cols] + jnp.zeros((T, LANES), jnp.float32)
            for k in range(FFN_CONV_WIDTH):
                off = SUBLANES - (FFN_CONV_WIDTH - 1) + k
                y = y + fcw_ref[k:k + 1, cols] * upbuf[slab, off:off + T, :]
            outs.append(y)
            upbuf[slab, 0:SUBLANES, :] = upbuf[slab, T:T + SUBLANES, :]
        return jnp.concatenate(outs, axis=-1)

    ups = up_proj(0)
    for c in range(N_FF_CHUNKS):
        nxt = up_proj(c + 1) if c + 1 < N_FF_CHUNKS else None
        g = conv3(ups[0], c * FF_CHUNK)
        val = conv3(ups[1], D_FF + c * FF_CHUNK)
        ups = nxt
        act = (g * jax.nn.sigmoid(g) * val).astype(jnp.bfloat16)
        contrib = jnp.dot(act, w_down_ref[c * FF_CHUNK:(c + 1) * FF_CHUNK, :],
                          preferred_element_type=jnp.float32)
        acc_ref[...] += contrib
    out_ref[0] = _rms(acc_ref[...], fin_g_ref[...]).astype(out_ref.dtype)


def _full(shape):
    return pl.BlockSpec(shape, lambda b, t: (0,) * len(shape))


def kernel(x, meta_tokens, mix_norm_g, w_in, q_norm_g, w_uq, kv_norm_g, w_ukv, conv_w, conv_b,
           conv_ln_g, conv_ln_b, conv_out_g, attn_out_g, w_out, ffn_norm_g, w_ffn_up,
           ffn_conv_w, ffn_conv_b, w_ffn_down, final_norm_g):
    B, S, D = x.shape
    assert D == D_MODEL and S % T == 0 and mix_norm_g.shape[0] == 1
    nt = S // T + 1
    lp = nt * T
    bf16, f32 = jnp.bfloat16, jnp.float32
    row2 = lambda v: v.reshape(1, -1).astype(f32)

    w_in0 = w_in[0]
    w_in_p = jnp.concatenate([
        w_in0[:, :KR_OFF],
        jnp.zeros((D, QK_NOPE), f32), w_in0[:, KR_OFF:], jnp.zeros((D, LANES - QK_NOPE - QK_ROPE), f32),
    ], axis=1).astype(bf16)
    w_uq3 = w_uq[0].reshape(Q_LORA, N_HEADS, QK_NOPE + QK_ROPE)
    w_uqt = jnp.pad(w_uq3, ((0, 0), (0, 0), (0, HEAD_PAD - QK_NOPE - QK_ROPE))).reshape(
        Q_LORA, N_HEADS * HEAD_PAD).T.astype(bf16)
    w_ukv3 = w_ukv[0].reshape(KV_LORA, N_HEADS, QK_NOPE + V_HEAD)
    w_uk = jnp.pad(w_ukv3[:, :, :QK_NOPE], ((0, 0), (0, 0), (0, HEAD_PAD - QK_NOPE))).reshape(
        KV_LORA, N_HEADS * HEAD_PAD).astype(bf16)
    w_uvt = w_ukv3[:, :, QK_NOPE:].reshape(KV_LORA, D_ATTN).T.astype(bf16)

    pos = (jnp.arange(lp, dtype=jnp.int32) - (FRONT - N_META)).astype(f32)
    inv_freq = 1.0 / (ROPE_THETA ** (jnp.arange(0, QK_ROPE, 2, dtype=f32) / QK_ROPE))
    ang = pos[:, None] * inv_freq[None, :]
    cos, sin = jnp.cos(ang), jnp.sin(ang)
    zl = lambda n: jnp.zeros((lp, n), f32)
    kc = jnp.concatenate([zl(QK_NOPE), cos, cos, zl(LANES - QK_NOPE - QK_ROPE)], axis=1)
    ks1 = jnp.concatenate([zl(QK_NOPE + ROPE_HALF), sin, zl(LANES - QK_NOPE - QK_ROPE)], axis=1)
    ks2 = jnp.concatenate([zl(QK_NOPE), -sin, zl(LANES - QK_NOPE - ROPE_HALF)], axis=1)
    qcos = (cos * Q_SCALE).T
    qsin = (sin * Q_SCALE).T

    x_spec = pl.BlockSpec((1, T, D), lambda b, t: (b, jnp.maximum(t - 1, 0), 0))
    params = pltpu.CompilerParams(dimension_semantics=("arbitrary", "arbitrary"),
                                  vmem_limit_bytes=VMEM_LIMIT)

    u_n, k4, qt4, vt4 = pl.pallas_call(
        _proj_kernel,
        grid=(B, nt),
        in_specs=[
            x_spec, _full((N_META, D)), _full((1, D)), _full((D, D_IN_PAD)),
            _full((1, Q_LORA)), _full((N_HEADS * HEAD_PAD, Q_LORA)), _full((1, KV_LORA)),
            _full((KV_LORA, N_HEADS * HEAD_PAD)), _full((D_ATTN, KV_LORA)),
            _full((CONV_WIDTH, D_CONV)), _full((1, D_CONV)), _full((1, D_CONV)),
            _full((1, D_CONV)), _full((1, D_CONV)),
            pl.BlockSpec((T, LANES), lambda b, t: (t, 0)),
            pl.BlockSpec((T, LANES), lambda b, t: (t, 0)),
            pl.BlockSpec((T, LANES), lambda b, t: (t, 0)),
            pl.BlockSpec((ROPE_HALF, T), lambda b, t: (0, t)),
            pl.BlockSpec((ROPE_HALF, T), lambda b, t: (0, t)),
        ],
        out_specs=[
            pl.BlockSpec((1, T, D_CONV), lambda b, t: (b, t, 0)),
            pl.BlockSpec((1, 1, T, N_HEADS * HEAD_PAD), lambda b, t: (b, t, 0, 0)),
            pl.BlockSpec((1, 1, N_HEADS * HEAD_PAD, T), lambda b, t: (b, t, 0, 0)),
            pl.BlockSpec((1, 1, D_ATTN, T), lambda b, t: (b, t, 0, 0)),
        ],
        out_shape=[
            jax.ShapeDtypeStruct((B, lp, D_CONV), bf16),
            jax.ShapeDtypeStruct((B, nt, T, N_HEADS * HEAD_PAD), bf16),
            jax.ShapeDtypeStruct((B, nt, N_HEADS * HEAD_PAD, T), bf16),
            jax.ShapeDtypeStruct((B, nt, D_ATTN, T), bf16),
        ],
        scratch_shapes=[pltpu.VMEM((D_CONV // LANES, HALO + T, LANES), f32)],
        compiler_params=params,
        name="proj_conv_qkv",
    )(x, meta_tokens.astype(f32), row2(mix_norm_g[0]), w_in_p, row2(q_norm_g[0]), w_uqt,
      row2(kv_norm_g[0]), w_uk, w_uvt, conv_w[0].astype(f32), row2(conv_b[0]),
      row2(conv_ln_g[0]), row2(conv_ln_b[0]), row2(conv_out_g[0]), kc, ks1, ks2, qcos, qsin)

    k_meta = k4[:, 0, T - N_META:, :]
    v_meta = jnp.pad(vt4[:, 0, :, T - N_META:],
                     ((0, 0), (0, 0), (0, LANES - N_META)))

    o_n = pl.pallas_call(
        _attn_kernel,
        grid=(B, nt),
        in_specs=[
            pl.BlockSpec((1, 1, N_HEADS * HEAD_PAD, T), lambda b, i: (b, i, 0, 0)),
            pl.BlockSpec((1, nt, T, N_HEADS * HEAD_PAD), lambda b, i: (b, 0, 0, 0)),
            pl.BlockSpec((1, nt, D_ATTN, T), lambda b, i: (b, 0, 0, 0)),
            pl.BlockSpec((1, N_META, N_HEADS * HEAD_PAD), lambda b, i: (b, 0, 0)),
            pl.BlockSpec((1, D_ATTN, LANES), lambda b, i: (b, 0, 0)),
            _full((1, D_ATTN)),
        ],
        out_specs=pl.BlockSpec((1, T, D_ATTN), lambda b, i: (b, i, 0)),
        out_shape=jax.ShapeDtypeStruct((B, lp, D_ATTN), bf16),
        scratch_shapes=[pltpu.VMEM((N_HEADS, 1, T), f32),
                        pltpu.VMEM((N_HEADS, 1, T), f32),
                        pltpu.VMEM((D_ATTN, T), f32),
                        pltpu.VMEM((2, N_HEADS, 1, T), f32),
                        pltpu.VMEM((2, N_HEADS, 1, T), f32),
                        pltpu.VMEM((2, N_HEADS, T, T), f32),
                        pltpu.VMEM((2, N_HEADS, T, T), bf16)],
        compiler_params=params,
        name="block_causal_attn",
    )(qt4, k4, vt4, k_meta, v_meta, row2(attn_out_g[0]))

    out = pl.pallas_call(
        _ffn_kernel,
        grid=(B, nt),
        in_specs=[
            x_spec, _full((N_META, D)),
            pl.BlockSpec((1, T, D_CONV), lambda b, t: (b, t, 0)),
            pl.BlockSpec((1, T, D_ATTN), lambda b, t: (b, t, 0)),
            _full((D_CONV + D_ATTN, D)), _full((1, D)), _full((D, 2 * D_FF)),
            _full((FFN_CONV_WIDTH, 2 * D_FF)), _full((1, 2 * D_FF)), _full((D_FF, D)),
            _full((1, D)),
        ],
        out_specs=pl.BlockSpec((1, T, D), lambda b, t: (b, jnp.maximum(t - 1, 0), 0)),
        out_shape=jax.ShapeDtypeStruct((B, S, D), x.dtype),
        scratch_shapes=[pltpu.VMEM((2 * D_FF // LANES, SUBLANES + T, LANES), f32),
                        pltpu.VMEM((T, D), f32)],
        compiler_params=params,
        name="outproj_convffn",
    )(x, meta_tokens.astype(f32), u_n, o_n, w_out[0].astype(bf16), row2(ffn_norm_g[0]),
      w_ffn_up[0].astype(bf16), ffn_conv_w[0].astype(f32), row2(ffn_conv_b[0]),
      w_ffn_down[0].astype(bf16), row2(final_norm_g))
    return out
```

```python
import jax
import jax.numpy as jnp
from jax import lax
from jax.experimental import pallas as pl
from jax.experimental.pallas import tpu as pltpu

D_MODEL = 1024
CHUNK = 64
N_META = 16
D_CONV = 512
CONV_WIDTH = 31
N_HEADS = 8
QK_NOPE = 64
QK_ROPE = 32
V_HEAD = 64
D_ATTN = N_HEADS * V_HEAD
Q_LORA = 384
KV_LORA = 256
ROPE_THETA = 10000.0
D_FF = 2816
FFN_CONV_WIDTH = 3
EPS = 1e-6
NEG = -1e30

LANES = 128
SUBLANES = 8
T = 256
T_FFN = 512
META_ROWS = 128
HEAD_PAD = 128
V_AUG = V_HEAD + 16
HALO = 32
CONV_ROWS = 64
ROPE_HALF = QK_ROPE // 2
D_IN_PAD = 2 * D_CONV + Q_LORA + KV_LORA + LANES
KR_OFF = 2 * D_CONV + Q_LORA + KV_LORA
N_CONV_SLABS = D_CONV // LANES
N_FF_SLABS = 2 * D_FF // LANES
FF_CHUNK = 256
N_FF_CHUNKS = D_FF // FF_CHUNK
FF_AHEAD = 2
VMEM_LIMIT = 56 * 1024 * 1024
Q_SCALE = (QK_NOPE + QK_ROPE) ** -0.5 * 1.4426950408889634
NT_DIMS = (((1,), (1,)), ((), ()))


def _rms(x, g):
    return x * lax.rsqrt(jnp.mean(x * x, axis=-1, keepdims=True) + EPS) * g


def _head_slices():
    return [(slice(hd * HEAD_PAD, (hd + 1) * HEAD_PAD), slice(hd * V_HEAD, (hd + 1) * V_HEAD),
             slice(hd * V_AUG, (hd + 1) * V_AUG)) for hd in range(N_HEADS)]


def _with_ones(vt):
    return jnp.concatenate([vt, jnp.ones((V_AUG - V_HEAD, vt.shape[1]), jnp.bfloat16)], axis=0)


def _conv_group(n, w_in_ref, xbuf, cw_ref, cb_ref, ln_g_ref, ln_b_ref, cog_ref):
    rows = n.shape[0]
    block = min(CONV_ROWS, rows)
    conv_slabs = []
    for s in range(N_CONV_SLABS):
        lanes = slice(s * LANES, (s + 1) * LANES)
        z = jnp.dot(n, w_in_ref[:, 2 * s * LANES:2 * (s + 1) * LANES],
                    preferred_element_type=jnp.float32)
        xbuf[s, HALO:HALO + rows, :] = z[:, :LANES] * jax.nn.sigmoid(z[:, LANES:])
        blocks = []
        for r0 in range(0, rows, block):
            acc = jnp.zeros((block, LANES), jnp.float32) + cb_ref[:, lanes]
            for k in range(CONV_WIDTH):
                off = HALO - (CONV_WIDTH - 1) + k + r0
                acc = acc + cw_ref[k:k + 1, lanes] * xbuf[s, off:off + block, :]
            blocks.append(acc)
        conv_slabs.append(jnp.concatenate(blocks, axis=0))
        xbuf[s, 0:HALO, :] = xbuf[s, rows:rows + HALO, :]
    c = jnp.concatenate(conv_slabs, axis=-1)
    mu = jnp.mean(c, axis=-1, keepdims=True)
    var = jnp.mean(jnp.square(c - mu), axis=-1, keepdims=True)
    c = (c - mu) * lax.rsqrt(var + EPS) * ln_g_ref[...] + ln_b_ref[...]
    c = c * jax.nn.sigmoid(c)
    return _rms(c, cog_ref[...]).astype(jnp.bfloat16)


def _qkv(n, w_in_ref, q_g_ref, w_uqt_ref, kv_g_ref, w_uk_ref, w_uvt_ref, kc, ks1, ks2, qcos, qsin):
    c_q = jnp.dot(n, w_in_ref[:, 2 * D_CONV:2 * D_CONV + Q_LORA],
                  preferred_element_type=jnp.float32)
    z_kv = jnp.dot(n, w_in_ref[:, 2 * D_CONV + Q_LORA:D_IN_PAD], preferred_element_type=jnp.float32)
    c_kv = z_kv[:, :KV_LORA]
    k_r = z_kv[:, KV_LORA:]
    qn = _rms(c_q, q_g_ref[...]).astype(jnp.bfloat16)
    kvn = _rms(c_kv, kv_g_ref[...]).astype(jnp.bfloat16)

    k_rot = (k_r * kc + pltpu.roll(k_r, ROPE_HALF, 1) * ks1
             + pltpu.roll(k_r, LANES - ROPE_HALF, 1) * ks2)
    k_nope = jnp.dot(kvn, w_uk_ref[...], preferred_element_type=jnp.float32)
    k = jnp.concatenate(
        [(k_nope[:, hd * HEAD_PAD:(hd + 1) * HEAD_PAD] + k_rot).astype(jnp.bfloat16)
         for hd in range(N_HEADS)], axis=-1)

    vt = lax.dot_general(w_uvt_ref[...], kvn, NT_DIMS,
                         preferred_element_type=jnp.float32).astype(jnp.bfloat16)
    qt = lax.dot_general(w_uqt_ref[...], qn, NT_DIMS, preferred_element_type=jnp.float32)
    pieces = []
    for hd in range(N_HEADS):
        b0 = hd * HEAD_PAD
        x1 = qt[b0 + QK_NOPE:b0 + QK_NOPE + ROPE_HALF]
        x2 = qt[b0 + QK_NOPE + ROPE_HALF:b0 + QK_NOPE + QK_ROPE]
        pieces += [qt[b0:b0 + QK_NOPE] * Q_SCALE, x1 * qcos - x2 * qsin, x2 * qcos + x1 * qsin,
                   jnp.zeros((HEAD_PAD - QK_NOPE - QK_ROPE, qt.shape[1]), jnp.float32)]
    return k, jnp.concatenate(pieces, axis=0).astype(jnp.bfloat16), vt


def _meta_keys_softmax(km, vm, qt):
    heads = _head_slices()
    nq = qt.shape[1]
    scores = [jnp.dot(km[:, qsl], qt[qsl, :], preferred_element_type=jnp.float32)
              for qsl, _, _ in heads]
    maxes, probs = [], []
    for s in scores:
        m = jnp.max(s, axis=0, keepdims=True)
        maxes.append(m)
        probs.append(jnp.concatenate(
            [jnp.exp2(s - m).astype(jnp.bfloat16),
             jnp.zeros((LANES - N_META, nq), jnp.bfloat16)], axis=0))
    accs = [jnp.dot(_with_ones(vm[vsl, :]), p, preferred_element_type=jnp.float32)
            for (_, vsl, _), p in zip(heads, probs)]
    return maxes, accs


def _attn_output(acc_of_head, g):
    ot = jnp.concatenate([a[:V_HEAD] / a[V_HEAD:V_HEAD + 1] for a in acc_of_head], axis=0)
    ot = ot * lax.rsqrt(jnp.mean(ot * ot, axis=0, keepdims=True) + EPS)
    return (ot.T * g).astype(jnp.bfloat16)


def _meta_kernel(meta_ref, mix_g_ref, w_in_ref, q_g_ref, w_uqt_ref, kv_g_ref, w_uk_ref, w_uvt_ref,
                 cw_ref, cb_ref, ln_g_ref, ln_b_ref, cog_ref, kc_ref, ks1_ref, ks2_ref,
                 qcos_ref, qsin_ref, attn_g_ref, w_out_ref, ffn_g_ref, w_up_ref,
                 halo_ref, km_ref, vm_ref, uph_ref, xbuf):
    h = jnp.concatenate(
        [meta_ref[...], jnp.zeros((META_ROWS - N_META, D_MODEL), jnp.float32)], axis=0)
    n = _rms(h, mix_g_ref[...]).astype(jnp.bfloat16)

    xbuf[:, 0:HALO, :] = jnp.zeros((N_CONV_SLABS, HALO, LANES), jnp.float32)
    u_n = _conv_group(n, w_in_ref, xbuf, cw_ref, cb_ref, ln_g_ref, ln_b_ref, cog_ref)
    halo_ref[:, 0:HALO - N_META, :] = jnp.zeros((N_CONV_SLABS, HALO - N_META, LANES), jnp.float32)
    halo_ref[:, HALO - N_META:HALO, :] = xbuf[:, HALO:HALO + N_META, :]

    k, qt, vt = _qkv(n, w_in_ref, q_g_ref, w_uqt_ref, kv_g_ref, w_uk_ref, w_uvt_ref,
                     kc_ref[...], ks1_ref[...], ks2_ref[...], qcos_ref[...], qsin_ref[...])
    km = k[0:N_META]
    key_is_meta = lax.broadcasted_iota(jnp.int32, (D_ATTN, META_ROWS), 1) < N_META
    vm = jnp.where(key_is_meta, vt, jnp.zeros_like(vt))
    km_ref[...] = km
    vm_ref[...] = vm

    _, accs = _meta_keys_softmax(km, vm, qt)
    o_n = _attn_output(accs, attn_g_ref[...])

    h1 = (h + jnp.dot(u_n, w_out_ref[0:D_CONV, :], preferred_element_type=jnp.float32)
          + jnp.dot(o_n, w_out_ref[D_CONV:, :], preferred_element_type=jnp.float32))
    n2 = _rms(h1, ffn_g_ref[...]).astype(jnp.bfloat16)
    for c in range(2 * D_FF // FF_CHUNK):
        up = jnp.dot(n2, w_up_ref[:, c * FF_CHUNK:(c + 1) * FF_CHUNK],
                     preferred_element_type=jnp.float32)
        for s in range(FF_CHUNK // LANES):
            uph_ref[c * (FF_CHUNK // LANES) + s] = up[N_META - SUBLANES:N_META,
                                                      s * LANES:(s + 1) * LANES]


def _proj_kernel(x_ref, halo_ref, mix_g_ref, w_in_ref, q_g_ref, w_uqt_ref, kv_g_ref,
                 w_uk_ref, w_uvt_ref, cw_ref, cb_ref, ln_g_ref, ln_b_ref, cog_ref,
                 kc_ref, ks1_ref, ks2_ref, qcos_ref, qsin_ref,
                 u_ref, k_ref, qt_ref, vt_ref, xbuf):
    @pl.when(pl.program_id(1) == 0)
    def _():
        xbuf[:, 0:HALO, :] = halo_ref[...]

    n = _rms(x_ref[0], mix_g_ref[...]).astype(jnp.bfloat16)
    u_ref[0] = _conv_group(n, w_in_ref, xbuf, cw_ref, cb_ref, ln_g_ref, ln_b_ref, cog_ref)
    k, qt, vt = _qkv(n, w_in_ref, q_g_ref, w_uqt_ref, kv_g_ref, w_uk_ref, w_uvt_ref,
                     kc_ref[...], ks1_ref[...], ks2_ref[...], qcos_ref[...], qsin_ref[...])
    k_ref[0, 0] = k
    qt_ref[0, 0] = qt
    vt_ref[0, 0] = vt


def _attn_kernel(qt_ref, k_ref, vt_ref, km_ref, vm_ref, g_ref, o_ref,
                 m_sc, acc_sc, alpha_sc, smax_sc, s_sc, p_sc):
    i = pl.program_id(1)
    heads = _head_slices()

    maxes, accs = _meta_keys_softmax(km_ref[...], vm_ref[...], qt_ref[0, 0])
    for hd, (_, _, asl) in enumerate(heads):
        m_sc[hd] = maxes[hd]
        acc_sc[asl, :] = accs[hd]

    def scores(n, slot):
        for hd, (qsl, _, _) in enumerate(heads):
            s = jnp.dot(k_ref[0, n - 1, :, qsl], qt_ref[0, 0, qsl, :],
                        preferred_element_type=jnp.float32)
            s_sc[slot, hd] = s
            smax_sc[slot, hd] = jnp.max(s, axis=0, keepdims=True)

    def softmax(slot, masked):
        if masked:
            key_chunk = lax.broadcasted_iota(jnp.int32, (T, T), 0) // CHUNK
            qry_chunk = lax.broadcasted_iota(jnp.int32, (T, T), 1) // CHUNK
            visible = key_chunk <= qry_chunk
        for hd in range(N_HEADS):
            s = s_sc[slot, hd]
            if masked:
                s = jnp.where(visible, s, NEG)
                tile_max = jnp.max(s, axis=0, keepdims=True)
            else:
                tile_max = smax_sc[slot, hd]
            m = m_sc[hd]
            m_new = jnp.maximum(m, tile_max)
            alpha_sc[slot, hd] = jnp.exp2(m - m_new)
            p_sc[slot, hd] = jnp.exp2(s - m_new).astype(jnp.bfloat16)
            m_sc[hd] = m_new

    def values(n, slot):
        for hd, (_, vsl, asl) in enumerate(heads):
            acc_sc[asl, :] = alpha_sc[slot, hd] * acc_sc[asl, :] + jnp.dot(
                _with_ones(vt_ref[0, jnp.maximum(n - 1, 0), vsl, :]), p_sc[slot, hd],
                preferred_element_type=jnp.float32)

    def stage(n, slot, masked, prefetch):
        softmax(slot, masked)
        if prefetch:
            scores(n + 1, 1 - slot)
        values(n - 1, 1 - slot)

    p_sc[0] = jnp.zeros((N_HEADS, T, T), jnp.bfloat16)
    alpha_sc[0] = jnp.ones((N_HEADS, 1, T), jnp.float32)
    scores(1, 1)
    n_last = i + 1
    n_pairs = i // 2

    def pair(kk, carry):
        n = 2 * kk + 1
        stage(n, 1, masked=False, prefetch=True)
        stage(n + 1, 0, masked=False, prefetch=True)
        return carry

    lax.fori_loop(0, n_pairs, pair, 0)
    n0 = 2 * n_pairs + 1

    @pl.when(n0 == n_last)
    def _():
        stage(n0, 1, masked=True, prefetch=False)
        values(n0, 1)

    @pl.when(n0 != n_last)
    def _():
        stage(n0, 1, masked=False, prefetch=True)
        stage(n0 + 1, 0, masked=True, prefetch=False)
        values(n0 + 1, 0)

    o_ref[0] = _attn_output([acc_sc[asl, :] for _, _, asl in heads], g_ref[...])


def _ffn_kernel(x_ref, uph_ref, u_ref, o_ref, w_out_ref, ffn_g_ref, w_up_ref, fcw_ref,
                fcb_ref, w_down_ref, fin_g_ref, out_ref, upbuf, acc_ref):
    t = pl.program_id(1)
    h1 = (x_ref[0]
          + jnp.dot(u_ref[0], w_out_ref[0:D_CONV, :], preferred_element_type=jnp.float32)
          + jnp.dot(o_ref[0], w_out_ref[D_CONV:, :], preferred_element_type=jnp.float32))
    acc_ref[...] = h1
    n2 = _rms(h1, ffn_g_ref[...]).astype(jnp.bfloat16)

    @pl.when(t == 0)
    def _():
        upbuf[:, 0:SUBLANES, :] = uph_ref[...]

    def up_proj(c):
        return tuple(
            jnp.dot(n2, w_up_ref[:, col0:col0 + FF_CHUNK], preferred_element_type=jnp.float32)
            for col0 in (c * FF_CHUNK, D_FF + c * FF_CHUNK))

    def conv3(up, col0):
        outs = []
        for s in range(FF_CHUNK // LANES):
            slab = col0 // LANES + s
            cols = slice(col0 + s * LANES, col0 + (s + 1) * LANES)
            upbuf[slab, SUBLANES:SUBLANES + T_FFN, :] = up[:, s * LANES:(s + 1) * LANES]
            y = fcb_ref[:, cols] + jnp.zeros((T_FFN, LANES), jnp.float32)
            for k in range(FFN_CONV_WIDTH):
                off = SUBLANES - (FFN_CONV_WIDTH - 1) + k
                y = y + fcw_ref[k:k + 1, cols] * upbuf[slab, off:off + T_FFN, :]
            outs.append(y)
            upbuf[slab, 0:SUBLANES, :] = upbuf[slab, T_FFN:T_FFN + SUBLANES, :]
        return jnp.concatenate(outs, axis=-1)

    ups = [up_proj(c) for c in range(FF_AHEAD)]
    for c in range(N_FF_CHUNKS):
        if c + FF_AHEAD < N_FF_CHUNKS:
            ups.append(up_proj(c + FF_AHEAD))
        up_g, up_val = ups.pop(0)
        g = conv3(up_g, c * FF_CHUNK)
        val = conv3(up_val, D_FF + c * FF_CHUNK)
        act = (g * jax.nn.sigmoid(g) * val).astype(jnp.bfloat16)
        contrib = jnp.dot(act, w_down_ref[c * FF_CHUNK:(c + 1) * FF_CHUNK, :],
                          preferred_element_type=jnp.float32)
        acc_ref[...] += contrib
    out_ref[0] = _rms(acc_ref[...], fin_g_ref[...]).astype(out_ref.dtype)


def _full(shape):
    return pl.BlockSpec(shape, lambda *_: (0,) * len(shape))


def _rope_tables(pos):
    f32 = jnp.float32
    inv_freq = 1.0 / (ROPE_THETA ** (jnp.arange(0, QK_ROPE, 2, dtype=f32) / QK_ROPE))
    ang = pos.astype(f32)[:, None] * inv_freq[None, :]
    cos, sin = jnp.cos(ang), jnp.sin(ang)
    zl = lambda n: jnp.zeros((pos.shape[0], n), f32)
    kc = jnp.concatenate([zl(QK_NOPE), cos, cos, zl(LANES - QK_NOPE - QK_ROPE)], axis=1)
    ks1 = jnp.concatenate([zl(QK_NOPE + ROPE_HALF), sin, zl(LANES - QK_NOPE - QK_ROPE)], axis=1)
    ks2 = jnp.concatenate([zl(QK_NOPE), -sin, zl(LANES - QK_NOPE - ROPE_HALF)], axis=1)
    return kc, ks1, ks2, (cos * Q_SCALE).T, (sin * Q_SCALE).T


def kernel(x, meta_tokens, mix_norm_g, w_in, q_norm_g, w_uq, kv_norm_g, w_ukv, conv_w, conv_b,
           conv_ln_g, conv_ln_b, conv_out_g, attn_out_g, w_out, ffn_norm_g, w_ffn_up,
           ffn_conv_w, ffn_conv_b, w_ffn_down, final_norm_g):
    B, S, D = x.shape
    assert D == D_MODEL and S % T == 0 and S % T_FFN == 0 and mix_norm_g.shape[0] == 1
    assert meta_tokens.shape == (N_META, D)
    nt = S // T
    bf16, f32 = jnp.bfloat16, jnp.float32
    row2 = lambda v: v.reshape(1, -1).astype(f32)

    w_in0 = w_in[0]
    glu_cols = [w_in0[:, g * D_CONV + s * LANES:g * D_CONV + (s + 1) * LANES]
                for s in range(N_CONV_SLABS) for g in (0, 1)]
    w_in_p = jnp.concatenate(glu_cols + [
        w_in0[:, 2 * D_CONV:KR_OFF],
        jnp.zeros((D, QK_NOPE), f32), w_in0[:, KR_OFF:], jnp.zeros((D, LANES - QK_NOPE - QK_ROPE), f32),
    ], axis=1).astype(bf16)
    w_uq3 = w_uq[0].reshape(Q_LORA, N_HEADS, QK_NOPE + QK_ROPE)
    w_uqt = jnp.pad(w_uq3, ((0, 0), (0, 0), (0, HEAD_PAD - QK_NOPE - QK_ROPE))).reshape(
        Q_LORA, N_HEADS * HEAD_PAD).T.astype(bf16)
    w_ukv3 = w_ukv[0].reshape(KV_LORA, N_HEADS, QK_NOPE + V_HEAD)
    w_uk = jnp.pad(w_ukv3[:, :, :QK_NOPE], ((0, 0), (0, 0), (0, HEAD_PAD - QK_NOPE))).reshape(
        KV_LORA, N_HEADS * HEAD_PAD).astype(bf16)
    w_uvt = w_ukv3[:, :, QK_NOPE:].reshape(KV_LORA, D_ATTN).T.astype(bf16)
    w_out_b = w_out[0].astype(bf16)
    w_up_b = w_ffn_up[0].astype(bf16)
    w_down_b = w_ffn_down[0].astype(bf16)
    mix_g, q_g, kv_g = row2(mix_norm_g[0]), row2(q_norm_g[0]), row2(kv_norm_g[0])
    cw, cb = conv_w[0].astype(f32), row2(conv_b[0])
    ln_g, ln_b, cog = row2(conv_ln_g[0]), row2(conv_ln_b[0]), row2(conv_out_g[0])
    attn_g, ffn_g = row2(attn_out_g[0]), row2(ffn_norm_g[0])

    meta_tabs = _rope_tables(jnp.arange(META_ROWS, dtype=jnp.int32))
    tabs = _rope_tables(jnp.arange(S, dtype=jnp.int32) + N_META)

    proj_w = [mix_g, w_in_p, q_g, w_uqt, kv_g, w_uk, w_uvt]
    proj_specs = [_full((1, D)), _full((D, D_IN_PAD)), _full((1, Q_LORA)),
                  _full((N_HEADS * HEAD_PAD, Q_LORA)), _full((1, KV_LORA)),
                  _full((KV_LORA, N_HEADS * HEAD_PAD)), _full((D_ATTN, KV_LORA))]
    conv_w_ = [cw, cb, ln_g, ln_b, cog]
    conv_specs = [_full((CONV_WIDTH, D_CONV)), _full((1, D_CONV)), _full((1, D_CONV)),
                  _full((1, D_CONV)), _full((1, D_CONV))]
    front_w, front_specs = proj_w + conv_w_, proj_specs + conv_specs
    seq = ("arbitrary", "arbitrary")

    conv_halo, k_meta, v_meta, up_halo = pl.pallas_call(
        _meta_kernel,
        grid=(1,),
        in_specs=[_full((N_META, D))] + front_specs + [
            _full((META_ROWS, LANES)), _full((META_ROWS, LANES)), _full((META_ROWS, LANES)),
            _full((ROPE_HALF, META_ROWS)), _full((ROPE_HALF, META_ROWS)),
            _full((1, D_ATTN)), _full((D_CONV + D_ATTN, D)), _full((1, D)), _full((D, 2 * D_FF))],
        out_specs=[_full((N_CONV_SLABS, HALO, LANES)), _full((N_META, N_HEADS * HEAD_PAD)),
                   _full((D_ATTN, META_ROWS)), _full((N_FF_SLABS, SUBLANES, LANES))],
        out_shape=[jax.ShapeDtypeStruct((N_CONV_SLABS, HALO, LANES), f32),
                   jax.ShapeDtypeStruct((N_META, N_HEADS * HEAD_PAD), bf16),
                   jax.ShapeDtypeStruct((D_ATTN, META_ROWS), bf16),
                   jax.ShapeDtypeStruct((N_FF_SLABS, SUBLANES, LANES), f32)],
        scratch_shapes=[pltpu.VMEM((N_CONV_SLABS, HALO + META_ROWS, LANES), f32)],
        compiler_params=pltpu.CompilerParams(dimension_semantics=("arbitrary",),
                                             vmem_limit_bytes=VMEM_LIMIT),
        name="meta_tokens",
    )(meta_tokens.astype(f32), *front_w, *meta_tabs, attn_g, w_out_b, ffn_g, w_up_b)

    params = pltpu.CompilerParams(dimension_semantics=seq, vmem_limit_bytes=VMEM_LIMIT)
    u_n, k4, qt4, vt4 = pl.pallas_call(
        _proj_kernel,
        grid=(B, nt),
        in_specs=[pl.BlockSpec((1, T, D), lambda b, t: (b, t, 0)),
                  _full((N_CONV_SLABS, HALO, LANES))] + front_specs + [
            pl.BlockSpec((T, LANES), lambda b, t: (t, 0)),
            pl.BlockSpec((T, LANES), lambda b, t: (t, 0)),
            pl.BlockSpec((T, LANES), lambda b, t: (t, 0)),
            pl.BlockSpec((ROPE_HALF, T), lambda b, t: (0, t)),
            pl.BlockSpec((ROPE_HALF, T), lambda b, t: (0, t))],
        out_specs=[
            pl.BlockSpec((1, T, D_CONV), lambda b, t: (b, t, 0)),
            pl.BlockSpec((1, 1, T, N_HEADS * HEAD_PAD), lambda b, t: (b, t, 0, 0)),
            pl.BlockSpec((1, 1, N_HEADS * HEAD_PAD, T), lambda b, t: (b, t, 0, 0)),
            pl.BlockSpec((1, 1, D_ATTN, T), lambda b, t: (b, t, 0, 0)),
        ],
        out_shape=[
            jax.ShapeDtypeStruct((B, S, D_CONV), bf16),
            jax.ShapeDtypeStruct((B, nt, T, N_HEADS * HEAD_PAD), bf16),
            jax.ShapeDtypeStruct((B, nt, N_HEADS * HEAD_PAD, T), bf16),
            jax.ShapeDtypeStruct((B, nt, D_ATTN, T), bf16),
        ],
        scratch_shapes=[pltpu.VMEM((N_CONV_SLABS, HALO + T, LANES), f32)],
        compiler_params=params,
        name="proj_conv_qkv",
    )(x, conv_halo, *front_w, *tabs)

    o_n = pl.pallas_call(
        _attn_kernel,
        grid=(B, nt),
        in_specs=[
            pl.BlockSpec((1, 1, N_HEADS * HEAD_PAD, T), lambda b, i: (b, i, 0, 0)),
            pl.BlockSpec((1, nt, T, N_HEADS * HEAD_PAD), lambda b, i: (b, 0, 0, 0)),
            pl.BlockSpec((1, nt, D_ATTN, T), lambda b, i: (b, 0, 0, 0)),
            _full((N_META, N_HEADS * HEAD_PAD)), _full((D_ATTN, META_ROWS)), _full((1, D_ATTN)),
        ],
        out_specs=pl.BlockSpec((1, T, D_ATTN), lambda b, i: (b, i, 0)),
        out_shape=jax.ShapeDtypeStruct((B, S, D_ATTN), bf16),
        scratch_shapes=[pltpu.VMEM((N_HEADS, 1, T), f32),
                        pltpu.VMEM((N_HEADS * V_AUG, T), f32),
                        pltpu.VMEM((2, N_HEADS, 1, T), f32),
                        pltpu.VMEM((2, N_HEADS, 1, T), f32),
                        pltpu.VMEM((2, N_HEADS, T, T), f32),
                        pltpu.VMEM((2, N_HEADS, T, T), bf16)],
        compiler_params=params,
        name="block_causal_attn",
    )(qt4, k4, vt4, k_meta, v_meta, attn_g)

    nf = S // T_FFN
    out = pl.pallas_call(
        _ffn_kernel,
        grid=(B, nf),
        in_specs=[
            pl.BlockSpec((1, T_FFN, D), lambda b, t: (b, t, 0)),
            _full((N_FF_SLABS, SUBLANES, LANES)),
            pl.BlockSpec((1, T_FFN, D_CONV), lambda b, t: (b, t, 0)),
            pl.BlockSpec((1, T_FFN, D_ATTN), lambda b, t: (b, t, 0)),
            _full((D_CONV + D_ATTN, D)), _full((1, D)), _full((D, 2 * D_FF)),
            _full((FFN_CONV_WIDTH, 2 * D_FF)), _full((1, 2 * D_FF)), _full((D_FF, D)),
            _full((1, D)),
        ],
        out_specs=pl.BlockSpec((1, T_FFN, D), lambda b, t: (b, t, 0)),
        out_shape=jax.ShapeDtypeStruct((B, S, D), x.dtype),
        scratch_shapes=[pltpu.VMEM((N_FF_SLABS, SUBLANES + T_FFN, LANES), f32),
                        pltpu.VMEM((T_FFN, D), f32)],
        compiler_params=params,
        name="outproj_convffn",
    )(x, up_halo, u_n, o_n, w_out_b, ffn_g, w_up_b, ffn_conv_w[0].astype(f32),
      row2(ffn_conv_b[0]), w_down_b, row2(final_norm_g))
    return out
```

```python
import jax
import jax.numpy as jnp
from jax import lax
from jax.experimental import pallas as pl
from jax.experimental.pallas import tpu as pltpu

D_MODEL = 1024
CHUNK = 64
N_META = 16
D_CONV = 512
CONV_WIDTH = 31
N_HEADS = 8
QK_NOPE = 64
QK_ROPE = 32
V_HEAD = 64
D_ATTN = N_HEADS * V_HEAD
Q_LORA = 384
KV_LORA = 256
ROPE_THETA = 10000.0
D_FF = 2816
FFN_CONV_WIDTH = 3
EPS = 1e-6
NEG = -1e30

LANES = 128
SUBLANES = 8
T = 256
T_FFN = 512
Q_TILES = 2
META_ROWS = 128
HEAD_PAD = 128
V_AUG = V_HEAD + 16
HALO = 32
CONV_ROWS = 64
ROPE_HALF = QK_ROPE // 2
D_IN_PAD = 2 * D_CONV + Q_LORA + KV_LORA + LANES
KR_OFF = 2 * D_CONV + Q_LORA + KV_LORA
N_CONV_SLABS = D_CONV // LANES
N_FF_SLABS = 2 * D_FF // LANES
FF_CHUNK = 256
N_FF_CHUNKS = D_FF // FF_CHUNK
FF_AHEAD = 2
VMEM_LIMIT = 56 * 1024 * 1024
Q_SCALE = (QK_NOPE + QK_ROPE) ** -0.5 * 1.4426950408889634
NT_DIMS = (((1,), (1,)), ((), ()))


def _rms(x, g):
    return x * lax.rsqrt(jnp.mean(x * x, axis=-1, keepdims=True) + EPS) * g


def _head_slices():
    return [(slice(hd * HEAD_PAD, (hd + 1) * HEAD_PAD), slice(hd * V_HEAD, (hd + 1) * V_HEAD),
             slice(hd * V_AUG, (hd + 1) * V_AUG)) for hd in range(N_HEADS)]


def _with_ones(vt):
    return jnp.concatenate([vt, jnp.ones((V_AUG - V_HEAD, vt.shape[1]), jnp.bfloat16)], axis=0)


def _conv_group(n, w_in_ref, xbuf, cw_ref, cb_ref, ln_g_ref, ln_b_ref, cog_ref):
    rows = n.shape[0]
    block = min(CONV_ROWS, rows)
    conv_slabs = []
    for s in range(N_CONV_SLABS):
        lanes = slice(s * LANES, (s + 1) * LANES)
        z = jnp.dot(n, w_in_ref[:, 2 * s * LANES:2 * (s + 1) * LANES],
                    preferred_element_type=jnp.float32)
        xbuf[s, HALO:HALO + rows, :] = z[:, :LANES] * jax.nn.sigmoid(z[:, LANES:])
        blocks = []
        for r0 in range(0, rows, block):
            acc = jnp.zeros((block, LANES), jnp.float32) + cb_ref[:, lanes]
            for k in range(CONV_WIDTH):
                off = HALO - (CONV_WIDTH - 1) + k + r0
                acc = acc + cw_ref[k:k + 1, lanes] * xbuf[s, off:off + block, :]
            blocks.append(acc)
        conv_slabs.append(jnp.concatenate(blocks, axis=0))
        xbuf[s, 0:HALO, :] = xbuf[s, rows:rows + HALO, :]
    c = jnp.concatenate(conv_slabs, axis=-1)
    mu = jnp.mean(c, axis=-1, keepdims=True)
    var = jnp.mean(jnp.square(c - mu), axis=-1, keepdims=True)
    c = (c - mu) * lax.rsqrt(var + EPS) * ln_g_ref[...] + ln_b_ref[...]
    c = c * jax.nn.sigmoid(c)
    return _rms(c, cog_ref[...]).astype(jnp.bfloat16)


def _qkv(n, w_in_ref, q_g_ref, w_uqt_ref, kv_g_ref, w_uk_ref, w_uvt_ref, kc, ks1, ks2, qcos, qsin):
    c_q = jnp.dot(n, w_in_ref[:, 2 * D_CONV:2 * D_CONV + Q_LORA],
                  preferred_element_type=jnp.float32)
    z_kv = jnp.dot(n, w_in_ref[:, 2 * D_CONV + Q_LORA:D_IN_PAD], preferred_element_type=jnp.float32)
    c_kv = z_kv[:, :KV_LORA]
    k_r = z_kv[:, KV_LORA:]
    qn = _rms(c_q, q_g_ref[...]).astype(jnp.bfloat16)
    kvn = _rms(c_kv, kv_g_ref[...]).astype(jnp.bfloat16)

    k_rot = (k_r * kc + pltpu.roll(k_r, ROPE_HALF, 1) * ks1
             + pltpu.roll(k_r, LANES - ROPE_HALF, 1) * ks2)
    k_nope = jnp.dot(kvn, w_uk_ref[...], preferred_element_type=jnp.float32)
    k = jnp.concatenate(
        [(k_nope[:, hd * HEAD_PAD:(hd + 1) * HEAD_PAD] + k_rot).astype(jnp.bfloat16)
         for hd in range(N_HEADS)], axis=-1)

    vt = lax.dot_general(w_uvt_ref[...], kvn, NT_DIMS,
                         preferred_element_type=jnp.float32).astype(jnp.bfloat16)
    qt = lax.dot_general(w_uqt_ref[...], qn, NT_DIMS, preferred_element_type=jnp.float32)
    pieces = []
    for hd in range(N_HEADS):
        b0 = hd * HEAD_PAD
        x1 = qt[b0 + QK_NOPE:b0 + QK_NOPE + ROPE_HALF]
        x2 = qt[b0 + QK_NOPE + ROPE_HALF:b0 + QK_NOPE + QK_ROPE]
        pieces += [qt[b0:b0 + QK_NOPE] * Q_SCALE, x1 * qcos - x2 * qsin, x2 * qcos + x1 * qsin,
                   jnp.zeros((HEAD_PAD - QK_NOPE - QK_ROPE, qt.shape[1]), jnp.float32)]
    return k, jnp.concatenate(pieces, axis=0).astype(jnp.bfloat16), vt


def _meta_keys_softmax(km, vm, qt):
    heads = _head_slices()
    nq = qt.shape[1]
    scores = [jnp.dot(km[:, qsl], qt[qsl, :], preferred_element_type=jnp.float32)
              for qsl, _, _ in heads]
    maxes, probs = [], []
    for s in scores:
        m = jnp.max(s, axis=0, keepdims=True)
        maxes.append(m)
        probs.append(jnp.concatenate(
            [jnp.exp2(s - m).astype(jnp.bfloat16),
             jnp.zeros((LANES - N_META, nq), jnp.bfloat16)], axis=0))
    accs = [jnp.dot(_with_ones(vm[vsl, :]), p, preferred_element_type=jnp.float32)
            for (_, vsl, _), p in zip(heads, probs)]
    return maxes, accs


def _attn_output(acc_of_head, g):
    ot = jnp.concatenate([a[:V_HEAD] / a[V_HEAD:V_HEAD + 1] for a in acc_of_head], axis=0)
    ot = ot * lax.rsqrt(jnp.mean(ot * ot, axis=0, keepdims=True) + EPS)
    return (ot.T * g).astype(jnp.bfloat16)


def _meta_kernel(meta_ref, mix_g_ref, w_in_ref, q_g_ref, w_uqt_ref, kv_g_ref, w_uk_ref, w_uvt_ref,
                 cw_ref, cb_ref, ln_g_ref, ln_b_ref, cog_ref, kc_ref, ks1_ref, ks2_ref,
                 qcos_ref, qsin_ref, attn_g_ref, w_out_ref, ffn_g_ref, w_up_ref,
                 halo_ref, km_ref, vm_ref, uph_ref, xbuf):
    h = jnp.concatenate(
        [meta_ref[...], jnp.zeros((META_ROWS - N_META, D_MODEL), jnp.float32)], axis=0)
    n = _rms(h, mix_g_ref[...]).astype(jnp.bfloat16)

    xbuf[:, 0:HALO, :] = jnp.zeros((N_CONV_SLABS, HALO, LANES), jnp.float32)
    u_n = _conv_group(n, w_in_ref, xbuf, cw_ref, cb_ref, ln_g_ref, ln_b_ref, cog_ref)
    halo_ref[:, 0:HALO - N_META, :] = jnp.zeros((N_CONV_SLABS, HALO - N_META, LANES), jnp.float32)
    halo_ref[:, HALO - N_META:HALO, :] = xbuf[:, HALO:HALO + N_META, :]

    k, qt, vt = _qkv(n, w_in_ref, q_g_ref, w_uqt_ref, kv_g_ref, w_uk_ref, w_uvt_ref,
                     kc_ref[...], ks1_ref[...], ks2_ref[...], qcos_ref[...], qsin_ref[...])
    km = k[0:N_META]
    key_is_meta = lax.broadcasted_iota(jnp.int32, (D_ATTN, META_ROWS), 1) < N_META
    vm = jnp.where(key_is_meta, vt, jnp.zeros_like(vt))
    km_ref[...] = km
    vm_ref[...] = vm

    _, accs = _meta_keys_softmax(km, vm, qt)
    o_n = _attn_output(accs, attn_g_ref[...])

    h1 = (h + jnp.dot(u_n, w_out_ref[0:D_CONV, :], preferred_element_type=jnp.float32)
          + jnp.dot(o_n, w_out_ref[D_CONV:, :], preferred_element_type=jnp.float32))
    n2 = _rms(h1, ffn_g_ref[...]).astype(jnp.bfloat16)
    for c in range(2 * D_FF // FF_CHUNK):
        up = jnp.dot(n2, w_up_ref[:, c * FF_CHUNK:(c + 1) * FF_CHUNK],
                     preferred_element_type=jnp.float32)
        for s in range(FF_CHUNK // LANES):
            uph_ref[c * (FF_CHUNK // LANES) + s] = up[N_META - SUBLANES:N_META,
                                                      s * LANES:(s + 1) * LANES]


def _proj_kernel(x_ref, halo_ref, mix_g_ref, w_in_ref, q_g_ref, w_uqt_ref, kv_g_ref,
                 w_uk_ref, w_uvt_ref, cw_ref, cb_ref, ln_g_ref, ln_b_ref, cog_ref,
                 kc_ref, ks1_ref, ks2_ref, qcos_ref, qsin_ref,
                 u_ref, k_ref, qt_ref, vt_ref, xbuf):
    @pl.when(pl.program_id(1) == 0)
    def _():
        xbuf[:, 0:HALO, :] = halo_ref[...]

    n = _rms(x_ref[0], mix_g_ref[...]).astype(jnp.bfloat16)
    u_ref[0] = _conv_group(n, w_in_ref, xbuf, cw_ref, cb_ref, ln_g_ref, ln_b_ref, cog_ref)
    k, qt, vt = _qkv(n, w_in_ref, q_g_ref, w_uqt_ref, kv_g_ref, w_uk_ref, w_uvt_ref,
                     kc_ref[...], ks1_ref[...], ks2_ref[...], qcos_ref[...], qsin_ref[...])
    k_ref[0, 0] = k
    qt_ref[0, 0] = qt
    vt_ref[0, 0] = vt


def _attn_kernel(qt_ref, k_ref, vt_ref, km_ref, vm_ref, g_ref, o_ref,
                 m_sc, acc_sc, alpha_sc, smax_sc, s_sc, p_sc):
    i = pl.program_id(1)
    heads = _head_slices()
    lanes = [(qh, qh * N_HEADS + hd, qsl, vsl, slice((qh * N_HEADS + hd) * V_AUG,
                                                   (qh * N_HEADS + hd + 1) * V_AUG))
             for qh in range(Q_TILES) for hd, (qsl, vsl, _) in enumerate(heads)]
    FULL, DIAG, SKIP = "full", "diag", "skip"
    all_full = [FULL] * len(lanes)
    tile_a = [DIAG if qh == 0 else FULL for qh, *_ in lanes]
    tile_b = [SKIP if qh == 0 else DIAG for qh, *_ in lanes]

    for qh in range(Q_TILES):
        maxes, accs = _meta_keys_softmax(km_ref[...], vm_ref[...], qt_ref[0, qh])
        for hd in range(N_HEADS):
            _, ln, _, _, asl = lanes[qh * N_HEADS + hd]
            m_sc[ln] = maxes[hd]
            acc_sc[asl, :] = accs[hd]

    def scores(n, slot, modes):
        for (qh, ln, qsl, _, _), mode in zip(lanes, modes):
            if mode == SKIP:
                continue
            s = jnp.dot(k_ref[0, n - 1, :, qsl], qt_ref[0, qh, qsl, :],
                        preferred_element_type=jnp.float32)
            s_sc[slot, ln] = s
            smax_sc[slot, ln] = jnp.max(s, axis=0, keepdims=True)

    def softmax(slot, modes):
        if DIAG in modes:
            key_chunk = lax.broadcasted_iota(jnp.int32, (T, T), 0) // CHUNK
            qry_chunk = lax.broadcasted_iota(jnp.int32, (T, T), 1) // CHUNK
            visible = key_chunk <= qry_chunk
        for (_, ln, _, _, _), mode in zip(lanes, modes):
            if mode == SKIP:
                continue
            s = s_sc[slot, ln]
            if mode == DIAG:
                s = jnp.where(visible, s, NEG)
                tile_max = jnp.max(s, axis=0, keepdims=True)
            else:
                tile_max = smax_sc[slot, ln]
            m = m_sc[ln]
            m_new = jnp.maximum(m, tile_max)
            alpha_sc[slot, ln] = jnp.exp2(m - m_new)
            p_sc[slot, ln] = jnp.exp2(s - m_new).astype(jnp.bfloat16)
            m_sc[ln] = m_new

    def values(n, slot, modes):
        for (_, ln, _, vsl, asl), mode in zip(lanes, modes):
            if mode == SKIP:
                continue
            acc_sc[asl, :] = alpha_sc[slot, ln] * acc_sc[asl, :] + jnp.dot(
                _with_ones(vt_ref[0, jnp.maximum(n - 1, 0), vsl, :]), p_sc[slot, ln],
                preferred_element_type=jnp.float32)

    def stage(n, slot, modes, prev_modes, next_modes):
        for qsel in range(Q_TILES):
            sel = lambda ms: [m if qh == qsel else SKIP for (qh, *_), m in zip(lanes, ms)]
            softmax(slot, sel(modes))
            if next_modes is not None:
                scores(n + 1, 1 - slot, sel(next_modes))
            values(n - 1, 1 - slot, sel(prev_modes))

    p_sc[0] = jnp.zeros((len(lanes), T, T), jnp.bfloat16)
    alpha_sc[0] = jnp.ones((len(lanes), 1, T), jnp.float32)
    scores(1, 1, all_full)

    def pair(kk, carry):
        n = 2 * kk + 1
        stage(n, 1, all_full, all_full, all_full)
        stage(n + 1, 0, all_full, all_full, all_full)
        return carry

    lax.fori_loop(0, i, pair, 0)
    n_a = 2 * i + 1
    stage(n_a, 1, tile_a, all_full, tile_b)
    stage(n_a + 1, 0, tile_b, tile_a, None)
    values(n_a + 1, 0, tile_b)

    for qh in range(Q_TILES):
        o_ref[0, qh * T:(qh + 1) * T, :] = _attn_output(
            [acc_sc[asl, :] for q, _, _, _, asl in lanes if q == qh], g_ref[...])


def _ffn_kernel(x_ref, uph_ref, u_ref, o_ref, w_out_ref, ffn_g_ref, w_up_ref, fcw_ref,
                fcb_ref, w_down_ref, fin_g_ref, out_ref, upbuf, acc_ref):
    t = pl.program_id(1)
    h1 = (x_ref[0]
          + jnp.dot(u_ref[0], w_out_ref[0:D_CONV, :], preferred_element_type=jnp.float32)
          + jnp.dot(o_ref[0], w_out_ref[D_CONV:, :], preferred_element_type=jnp.float32))
    acc_ref[...] = h1
    n2 = _rms(h1, ffn_g_ref[...]).astype(jnp.bfloat16)

    @pl.when(t == 0)
    def _():
        upbuf[:, 0:SUBLANES, :] = uph_ref[...]

    def up_proj(c):
        return tuple(
            jnp.dot(n2, w_up_ref[:, col0:col0 + FF_CHUNK], preferred_element_type=jnp.float32)
            for col0 in (c * FF_CHUNK, D_FF + c * FF_CHUNK))

    def conv3(up, col0):
        outs = []
        for s in range(FF_CHUNK // LANES):
            slab = col0 // LANES + s
            cols = slice(col0 + s * LANES, col0 + (s + 1) * LANES)
            upbuf[slab, SUBLANES:SUBLANES + T_FFN, :] = up[:, s * LANES:(s + 1) * LANES]
            y = fcb_ref[:, cols] + jnp.zeros((T_FFN, LANES), jnp.float32)
            for k in range(FFN_CONV_WIDTH):
                off = SUBLANES - (FFN_CONV_WIDTH - 1) + k
                y = y + fcw_ref[k:k + 1, cols] * upbuf[slab, off:off + T_FFN, :]
            outs.append(y)
            upbuf[slab, 0:SUBLANES, :] = upbuf[slab, T_FFN:T_FFN + SUBLANES, :]
        return jnp.concatenate(outs, axis=-1)

    ups = [up_proj(c) for c in range(FF_AHEAD)]
    for c in range(N_FF_CHUNKS):
        if c + FF_AHEAD < N_FF_CHUNKS:
            ups.append(up_proj(c + FF_AHEAD))
        up_g, up_val = ups.pop(0)
        g = conv3(up_g, c * FF_CHUNK)
        val = conv3(up_val, D_FF + c * FF_CHUNK)
        act = (g * jax.nn.sigmoid(g) * val).astype(jnp.bfloat16)
        contrib = jnp.dot(act, w_down_ref[c * FF_CHUNK:(c + 1) * FF_CHUNK, :],
                          preferred_element_type=jnp.float32)
        acc_ref[...] += contrib
    out_ref[0] = _rms(acc_ref[...], fin_g_ref[...]).astype(out_ref.dtype)


def _full(shape):
    return pl.BlockSpec(shape, lambda *_: (0,) * len(shape))


def _rope_tables(pos):
    f32 = jnp.float32
    inv_freq = 1.0 / (ROPE_THETA ** (jnp.arange(0, QK_ROPE, 2, dtype=f32) / QK_ROPE))
    ang = pos.astype(f32)[:, None] * inv_freq[None, :]
    cos, sin = jnp.cos(ang), jnp.sin(ang)
    zl = lambda n: jnp.zeros((pos.shape[0], n), f32)
    kc = jnp.concatenate([zl(QK_NOPE), cos, cos, zl(LANES - QK_NOPE - QK_ROPE)], axis=1)
    ks1 = jnp.concatenate([zl(QK_NOPE + ROPE_HALF), sin, zl(LANES - QK_NOPE - QK_ROPE)], axis=1)
    ks2 = jnp.concatenate([zl(QK_NOPE), -sin, zl(LANES - QK_NOPE - ROPE_HALF)], axis=1)
    return kc, ks1, ks2, (cos * Q_SCALE).T, (sin * Q_SCALE).T


def kernel(x, meta_tokens, mix_norm_g, w_in, q_norm_g, w_uq, kv_norm_g, w_ukv, conv_w, conv_b,
           conv_ln_g, conv_ln_b, conv_out_g, attn_out_g, w_out, ffn_norm_g, w_ffn_up,
           ffn_conv_w, ffn_conv_b, w_ffn_down, final_norm_g):
    B, S, D = x.shape
    assert D == D_MODEL and S % (Q_TILES * T) == 0 and S % T_FFN == 0 and mix_norm_g.shape[0] == 1
    assert meta_tokens.shape == (N_META, D)
    nt = S // T
    bf16, f32 = jnp.bfloat16, jnp.float32
    row2 = lambda v: v.reshape(1, -1).astype(f32)

    w_in0 = w_in[0]
    glu_cols = [w_in0[:, g * D_CONV + s * LANES:g * D_CONV + (s + 1) * LANES]
                for s in range(N_CONV_SLABS) for g in (0, 1)]
    w_in_p = jnp.concatenate(glu_cols + [
        w_in0[:, 2 * D_CONV:KR_OFF],
        jnp.zeros((D, QK_NOPE), f32), w_in0[:, KR_OFF:], jnp.zeros((D, LANES - QK_NOPE - QK_ROPE), f32),
    ], axis=1).astype(bf16)
    w_uq3 = w_uq[0].reshape(Q_LORA, N_HEADS, QK_NOPE + QK_ROPE)
    w_uqt = jnp.pad(w_uq3, ((0, 0), (0, 0), (0, HEAD_PAD - QK_NOPE - QK_ROPE))).reshape(
        Q_LORA, N_HEADS * HEAD_PAD).T.astype(bf16)
    w_ukv3 = w_ukv[0].reshape(KV_LORA, N_HEADS, QK_NOPE + V_HEAD)
    w_uk = jnp.pad(w_ukv3[:, :, :QK_NOPE], ((0, 0), (0, 0), (0, HEAD_PAD - QK_NOPE))).reshape(
        KV_LORA, N_HEADS * HEAD_PAD).astype(bf16)
    w_uvt = w_ukv3[:, :, QK_NOPE:].reshape(KV_LORA, D_ATTN).T.astype(bf16)
    w_out_b = w_out[0].astype(bf16)
    w_up_b = w_ffn_up[0].astype(bf16)
    w_down_b = w_ffn_down[0].astype(bf16)
    mix_g, q_g, kv_g = row2(mix_norm_g[0]), row2(q_norm_g[0]), row2(kv_norm_g[0])
    cw, cb = conv_w[0].astype(f32), row2(conv_b[0])
    ln_g, ln_b, cog = row2(conv_ln_g[0]), row2(conv_ln_b[0]), row2(conv_out_g[0])
    attn_g, ffn_g = row2(attn_out_g[0]), row2(ffn_norm_g[0])

    meta_tabs = _rope_tables(jnp.arange(META_ROWS, dtype=jnp.int32))
    tabs = _rope_tables(jnp.arange(S, dtype=jnp.int32) + N_META)

    proj_w = [mix_g, w_in_p, q_g, w_uqt, kv_g, w_uk, w_uvt]
    proj_specs = [_full((1, D)), _full((D, D_IN_PAD)), _full((1, Q_LORA)),
                  _full((N_HEADS * HEAD_PAD, Q_LORA)), _full((1, KV_LORA)),
                  _full((KV_LORA, N_HEADS * HEAD_PAD)), _full((D_ATTN, KV_LORA))]
    conv_w_ = [cw, cb, ln_g, ln_b, cog]
    conv_specs = [_full((CONV_WIDTH, D_CONV)), _full((1, D_CONV)), _full((1, D_CONV)),
                  _full((1, D_CONV)), _full((1, D_CONV))]
    front_w, front_specs = proj_w + conv_w_, proj_specs + conv_specs
    seq = ("arbitrary", "arbitrary")

    conv_halo, k_meta, v_meta, up_halo = pl.pallas_call(
        _meta_kernel,
        grid=(1,),
        in_specs=[_full((N_META, D))] + front_specs + [
            _full((META_ROWS, LANES)), _full((META_ROWS, LANES)), _full((META_ROWS, LANES)),
            _full((ROPE_HALF, META_ROWS)), _full((ROPE_HALF, META_ROWS)),
            _full((1, D_ATTN)), _full((D_CONV + D_ATTN, D)), _full((1, D)), _full((D, 2 * D_FF))],
        out_specs=[_full((N_CONV_SLABS, HALO, LANES)), _full((N_META, N_HEADS * HEAD_PAD)),
                   _full((D_ATTN, META_ROWS)), _full((N_FF_SLABS, SUBLANES, LANES))],
        out_shape=[jax.ShapeDtypeStruct((N_CONV_SLABS, HALO, LANES), f32),
                   jax.ShapeDtypeStruct((N_META, N_HEADS * HEAD_PAD), bf16),
                   jax.ShapeDtypeStruct((D_ATTN, META_ROWS), bf16),
                   jax.ShapeDtypeStruct((N_FF_SLABS, SUBLANES, LANES), f32)],
        scratch_shapes=[pltpu.VMEM((N_CONV_SLABS, HALO + META_ROWS, LANES), f32)],
        compiler_params=pltpu.CompilerParams(dimension_semantics=("arbitrary",),
                                             vmem_limit_bytes=VMEM_LIMIT),
        name="meta_tokens",
    )(meta_tokens.astype(f32), *front_w, *meta_tabs, attn_g, w_out_b, ffn_g, w_up_b)

    params = pltpu.CompilerParams(dimension_semantics=seq, vmem_limit_bytes=VMEM_LIMIT)
    u_n, k4, qt4, vt4 = pl.pallas_call(
        _proj_kernel,
        grid=(B, nt),
        in_specs=[pl.BlockSpec((1, T, D), lambda b, t: (b, t, 0)),
                  _full((N_CONV_SLABS, HALO, LANES))] + front_specs + [
            pl.BlockSpec((T, LANES), lambda b, t: (t, 0)),
            pl.BlockSpec((T, LANES), lambda b, t: (t, 0)),
            pl.BlockSpec((T, LANES), lambda b, t: (t, 0)),
            pl.BlockSpec((ROPE_HALF, T), lambda b, t: (0, t)),
            pl.BlockSpec((ROPE_HALF, T), lambda b, t: (0, t))],
        out_specs=[
            pl.BlockSpec((1, T, D_CONV), lambda b, t: (b, t, 0)),
            pl.BlockSpec((1, 1, T, N_HEADS * HEAD_PAD), lambda b, t: (b, t, 0, 0)),
            pl.BlockSpec((1, 1, N_HEADS * HEAD_PAD, T), lambda b, t: (b, t, 0, 0)),
            pl.BlockSpec((1, 1, D_ATTN, T), lambda b, t: (b, t, 0, 0)),
        ],
        out_shape=[
            jax.ShapeDtypeStruct((B, S, D_CONV), bf16),
            jax.ShapeDtypeStruct((B, nt, T, N_HEADS * HEAD_PAD), bf16),
            jax.ShapeDtypeStruct((B, nt, N_HEADS * HEAD_PAD, T), bf16),
            jax.ShapeDtypeStruct((B, nt, D_ATTN, T), bf16),
        ],
        scratch_shapes=[pltpu.VMEM((N_CONV_SLABS, HALO + T, LANES), f32)],
        compiler_params=params,
        name="proj_conv_qkv",
    )(x, conv_halo, *front_w, *tabs)

    n_lanes = Q_TILES * N_HEADS
    o_n = pl.pallas_call(
        _attn_kernel,
        grid=(B, nt // Q_TILES),
        in_specs=[
            pl.BlockSpec((1, Q_TILES, N_HEADS * HEAD_PAD, T), lambda b, i: (b, i, 0, 0)),
            pl.BlockSpec((1, nt, T, N_HEADS * HEAD_PAD), lambda b, i: (b, 0, 0, 0)),
            pl.BlockSpec((1, nt, D_ATTN, T), lambda b, i: (b, 0, 0, 0)),
            _full((N_META, N_HEADS * HEAD_PAD)), _full((D_ATTN, META_ROWS)), _full((1, D_ATTN)),
        ],
        out_specs=pl.BlockSpec((1, Q_TILES * T, D_ATTN), lambda b, i: (b, i, 0)),
        out_shape=jax.ShapeDtypeStruct((B, S, D_ATTN), bf16),
        scratch_shapes=[pltpu.VMEM((n_lanes, 1, T), f32),
                        pltpu.VMEM((n_lanes * V_AUG, T), f32),
                        pltpu.VMEM((2, n_lanes, 1, T), f32),
                        pltpu.VMEM((2, n_lanes, 1, T), f32),
                        pltpu.VMEM((2, n_lanes, T, T), f32),
                        pltpu.VMEM((2, n_lanes, T, T), bf16)],
        compiler_params=params,
        name="block_causal_attn",
    )(qt4, k4, vt4, k_meta, v_meta, attn_g)

    nf = S // T_FFN
    out = pl.pallas_call(
        _ffn_kernel,
        grid=(B, nf),
        in_specs=[
            pl.BlockSpec((1, T_FFN, D), lambda b, t: (b, t, 0)),
            _full((N_FF_SLABS, SUBLANES, LANES)),
            pl.BlockSpec((1, T_FFN, D_CONV), lambda b, t: (b, t, 0)),
            pl.BlockSpec((1, T_FFN, D_ATTN), lambda b, t: (b, t, 0)),
            _full((D_CONV + D_ATTN, D)), _full((1, D)), _full((D, 2 * D_FF)),
            _full((FFN_CONV_WIDTH, 2 * D_FF)), _full((1, 2 * D_FF)), _full((D_FF, D)),
            _full((1, D)),
        ],
        out_specs=pl.BlockSpec((1, T_FFN, D), lambda b, t: (b, t, 0)),
        out_shape=jax.ShapeDtypeStruct((B, S, D), x.dtype),
        scratch_shapes=[pltpu.VMEM((N_FF_SLABS, SUBLANES + T_FFN, LANES), f32),
                        pltpu.VMEM((T_FFN, D), f32)],
        compiler_params=params,
        name="outproj_convffn",
    )(x, up_halo, u_n, o_n, w_out_b, ffn_g, w_up_b, ffn_conv_w[0].astype(f32),
      row2(ffn_conv_b[0]), w_down_b, row2(final_norm_g))
    return out
```

```python
import jax
import jax.numpy as jnp
import numpy as np
from jax import lax
from jax.experimental import pallas as pl
from jax.experimental.pallas import tpu as pltpu

D_MODEL = 1024
CHUNK = 64
N_META = 16
D_CONV = 512
CONV_WIDTH = 31
N_HEADS = 8
QK_NOPE = 64
QK_ROPE = 32
V_HEAD = 64
D_ATTN = N_HEADS * V_HEAD
Q_LORA = 384
KV_LORA = 256
ROPE_THETA = 10000.0
D_FF = 2816
FFN_CONV_WIDTH = 3
EPS = 1e-6
NEG = -1e30

LANES = 128
SUBLANES = 8
T = 256
T_PROJ = 512
T_FFN = 512
Q_TILES = 2
META_ROWS = 128
HEAD_PAD = 128
V_AUG = V_HEAD + 16
HALO = 32
CONV_ROWS = 64
ROPE_HALF = QK_ROPE // 2
D_IN_PAD = 2 * D_CONV + Q_LORA + KV_LORA + LANES
KR_OFF = 2 * D_CONV + Q_LORA + KV_LORA
N_CONV_SLABS = D_CONV // LANES
N_FF_SLABS = 2 * D_FF // LANES
FF_CHUNK = 256
N_FF_CHUNKS = D_FF // FF_CHUNK
FF_AHEAD = 3
VMEM_LIMIT = 56 * 1024 * 1024
Q_SCALE = (QK_NOPE + QK_ROPE) ** -0.5 * 1.4426950408889634
NT_DIMS = (((1,), (1,)), ((), ()))


def _rms(x, g):
    return x * lax.rsqrt(jnp.mean(x * x, axis=-1, keepdims=True) + EPS) * g


def _head_slices():
    return [(slice(hd * HEAD_PAD, (hd + 1) * HEAD_PAD), slice(hd * V_HEAD, (hd + 1) * V_HEAD),
             slice(hd * V_AUG, (hd + 1) * V_AUG)) for hd in range(N_HEADS)]


def _with_ones(vt):
    return jnp.concatenate([vt, jnp.ones((V_AUG - V_HEAD, vt.shape[1]), jnp.bfloat16)], axis=0)


def _conv_group(n, w_in_ref, xbuf, cw_ref, cb_ref, ln_g_ref, ln_b_ref, cog_ref):
    rows = n.shape[0]
    block = min(CONV_ROWS, rows)
    conv_slabs = []
    for s in range(N_CONV_SLABS):
        lanes = slice(s * LANES, (s + 1) * LANES)
        z = jnp.dot(n, w_in_ref[:, 2 * s * LANES:2 * (s + 1) * LANES],
                    preferred_element_type=jnp.float32)
        xbuf[s, HALO:HALO + rows, :] = z[:, :LANES] * jax.nn.sigmoid(z[:, LANES:])
        blocks = []
        for r0 in range(0, rows, block):
            acc = jnp.zeros((block, LANES), jnp.float32) + cb_ref[:, lanes]
            for k in range(CONV_WIDTH):
                off = HALO - (CONV_WIDTH - 1) + k + r0
                acc = acc + cw_ref[k:k + 1, lanes] * xbuf[s, off:off + block, :]
            blocks.append(acc)
        conv_slabs.append(jnp.concatenate(blocks, axis=0))
        xbuf[s, 0:HALO, :] = xbuf[s, rows:rows + HALO, :]
    c = jnp.concatenate(conv_slabs, axis=-1)
    mu = jnp.mean(c, axis=-1, keepdims=True)
    var = jnp.mean(jnp.square(c - mu), axis=-1, keepdims=True)
    c = (c - mu) * lax.rsqrt(var + EPS) * ln_g_ref[...] + ln_b_ref[...]
    c = c * jax.nn.sigmoid(c)
    return _rms(c, cog_ref[...]).astype(jnp.bfloat16)


def _qkv(n, w_in_ref, q_g_ref, w_uqt_ref, kv_g_ref, w_uk_ref, w_uvt_ref, kc, ks1, ks2, qcos, qsin):
    c_q = jnp.dot(n, w_in_ref[:, 2 * D_CONV:2 * D_CONV + Q_LORA],
                  preferred_element_type=jnp.float32)
    z_kv = jnp.dot(n, w_in_ref[:, 2 * D_CONV + Q_LORA:D_IN_PAD], preferred_element_type=jnp.float32)
    c_kv = z_kv[:, :KV_LORA]
    k_r = z_kv[:, KV_LORA:]
    qn = _rms(c_q, q_g_ref[...]).astype(jnp.bfloat16)
    kvn = _rms(c_kv, kv_g_ref[...]).astype(jnp.bfloat16)

    k_rot = (k_r * kc + pltpu.roll(k_r, ROPE_HALF, 1) * ks1
             + pltpu.roll(k_r, LANES - ROPE_HALF, 1) * ks2)
    k_nope = jnp.dot(kvn, w_uk_ref[...], preferred_element_type=jnp.float32)
    k = jnp.concatenate(
        [(k_nope[:, hd * HEAD_PAD:(hd + 1) * HEAD_PAD] + k_rot).astype(jnp.bfloat16)
         for hd in range(N_HEADS)], axis=-1)

    vt = lax.dot_general(w_uvt_ref[...], kvn, NT_DIMS,
                         preferred_element_type=jnp.float32).astype(jnp.bfloat16)
    qt = lax.dot_general(w_uqt_ref[...], qn, NT_DIMS, preferred_element_type=jnp.float32)
    pieces = []
    for hd in range(N_HEADS):
        b0 = hd * HEAD_PAD
        x1 = qt[b0 + QK_NOPE:b0 + QK_NOPE + ROPE_HALF]
        x2 = qt[b0 + QK_NOPE + ROPE_HALF:b0 + QK_NOPE + QK_ROPE]
        pieces += [qt[b0:b0 + QK_NOPE] * Q_SCALE, x1 * qcos - x2 * qsin, x2 * qcos + x1 * qsin,
                   jnp.zeros((HEAD_PAD - QK_NOPE - QK_ROPE, qt.shape[1]), jnp.float32)]
    return k, jnp.concatenate(pieces, axis=0).astype(jnp.bfloat16), vt


def _meta_keys_softmax(km, vm, qt):
    heads = _head_slices()
    nq = qt.shape[1]
    scores = [jnp.dot(km[:, qsl], qt[qsl, :], preferred_element_type=jnp.float32)
              for qsl, _, _ in heads]
    maxes, probs = [], []
    for s in scores:
        m = jnp.max(s, axis=0, keepdims=True)
        maxes.append(m)
        probs.append(jnp.concatenate(
            [jnp.exp2(s - m).astype(jnp.bfloat16),
             jnp.zeros((LANES - N_META, nq), jnp.bfloat16)], axis=0))
    accs = [jnp.dot(_with_ones(vm[vsl, :]), p, preferred_element_type=jnp.float32)
            for (_, vsl, _), p in zip(heads, probs)]
    return maxes, accs


def _attn_output(acc_of_head, g):
    ot = jnp.concatenate([a[:V_HEAD] / a[V_HEAD:V_HEAD + 1] for a in acc_of_head], axis=0)
    ot = ot * lax.rsqrt(jnp.mean(ot * ot, axis=0, keepdims=True) + EPS)
    return (ot.T * g).astype(jnp.bfloat16)


def _meta_kernel(meta_ref, mix_g_ref, w_in_ref, q_g_ref, w_uqt_ref, kv_g_ref, w_uk_ref, w_uvt_ref,
                 cw_ref, cb_ref, ln_g_ref, ln_b_ref, cog_ref, kc_ref, ks1_ref, ks2_ref,
                 qcos_ref, qsin_ref, attn_g_ref, w_out_ref, ffn_g_ref, w_up_ref,
                 halo_ref, km_ref, vm_ref, uph_ref, xbuf):
    h = jnp.concatenate(
        [meta_ref[...], jnp.zeros((META_ROWS - N_META, D_MODEL), jnp.float32)], axis=0)
    n = _rms(h, mix_g_ref[...]).astype(jnp.bfloat16)

    xbuf[:, 0:HALO, :] = jnp.zeros((N_CONV_SLABS, HALO, LANES), jnp.float32)
    u_n = _conv_group(n, w_in_ref, xbuf, cw_ref, cb_ref, ln_g_ref, ln_b_ref, cog_ref)
    halo_ref[:, 0:HALO - N_META, :] = jnp.zeros((N_CONV_SLABS, HALO - N_META, LANES), jnp.float32)
    halo_ref[:, HALO - N_META:HALO, :] = xbuf[:, HALO:HALO + N_META, :]

    k, qt, vt = _qkv(n, w_in_ref, q_g_ref, w_uqt_ref, kv_g_ref, w_uk_ref, w_uvt_ref,
                     kc_ref[...], ks1_ref[...], ks2_ref[...], qcos_ref[...], qsin_ref[...])
    km = k[0:N_META]
    key_is_meta = lax.broadcasted_iota(jnp.int32, (D_ATTN, META_ROWS), 1) < N_META
    vm = jnp.where(key_is_meta, vt, jnp.zeros_like(vt))
    km_ref[...] = km
    vm_ref[...] = vm

    _, accs = _meta_keys_softmax(km, vm, qt)
    o_n = _attn_output(accs, attn_g_ref[...])

    h1 = (h + jnp.dot(u_n, w_out_ref[0:D_CONV, :], preferred_element_type=jnp.float32)
          + jnp.dot(o_n, w_out_ref[D_CONV:, :], preferred_element_type=jnp.float32))
    n2 = _rms(h1, ffn_g_ref[...]).astype(jnp.bfloat16)
    for c in range(2 * D_FF // FF_CHUNK):
        up = jnp.dot(n2, w_up_ref[:, c * FF_CHUNK:(c + 1) * FF_CHUNK],
                     preferred_element_type=jnp.float32)
        for s in range(FF_CHUNK // LANES):
            uph_ref[c * (FF_CHUNK // LANES) + s] = up[N_META - SUBLANES:N_META,
                                                      s * LANES:(s + 1) * LANES]


def _proj_kernel(x_ref, halo_ref, mix_g_ref, w_in_ref, q_g_ref, w_uqt_ref, kv_g_ref,
                 w_uk_ref, w_uvt_ref, cw_ref, cb_ref, ln_g_ref, ln_b_ref, cog_ref,
                 kc_ref, ks1_ref, ks2_ref, qcos_ref, qsin_ref,
                 u_ref, k_ref, qt_ref, vt_ref, xbuf):
    @pl.when(pl.program_id(1) == 0)
    def _():
        xbuf[:, 0:HALO, :] = halo_ref[...]

    n = _rms(x_ref[0], mix_g_ref[...]).astype(jnp.bfloat16)
    u_ref[0] = _conv_group(n, w_in_ref, xbuf, cw_ref, cb_ref, ln_g_ref, ln_b_ref, cog_ref)
    k, qt, vt = _qkv(n, w_in_ref, q_g_ref, w_uqt_ref, kv_g_ref, w_uk_ref, w_uvt_ref,
                     kc_ref[...], ks1_ref[...], ks2_ref[...], qcos_ref[...], qsin_ref[...])
    for j in range(T_PROJ // T):
        k_ref[0, j] = k[j * T:(j + 1) * T]
        qt_ref[0, j] = qt[:, j * T:(j + 1) * T]
        vt_ref[0, j] = vt[:, j * T:(j + 1) * T]


def _attn_kernel(qt_ref, k_ref, vt_ref, km_ref, vm_ref, g_ref, o_ref,
                 m_sc, acc_sc, alpha_sc, smax_sc, s_sc, p_sc):
    i = pl.program_id(1)
    heads = _head_slices()
    lanes = [(qh, qh * N_HEADS + hd, qsl, vsl, slice((qh * N_HEADS + hd) * V_AUG,
                                                   (qh * N_HEADS + hd + 1) * V_AUG))
             for qh in range(Q_TILES) for hd, (qsl, vsl, _) in enumerate(heads)]
    FULL, DIAG, SKIP = "full", "diag", "skip"
    all_full = [FULL] * len(lanes)
    tile_a = [DIAG if qh == 0 else FULL for qh, *_ in lanes]
    tile_b = [SKIP if qh == 0 else DIAG for qh, *_ in lanes]

    for qh in range(Q_TILES):
        maxes, accs = _meta_keys_softmax(km_ref[...], vm_ref[...], qt_ref[0, qh])
        for hd in range(N_HEADS):
            _, ln, _, _, asl = lanes[qh * N_HEADS + hd]
            m_sc[ln] = maxes[hd]
            acc_sc[asl, :] = accs[hd]

    def scores(n, slot, modes):
        for (qh, ln, qsl, _, _), mode in zip(lanes, modes):
            if mode == SKIP:
                continue
            s = jnp.dot(k_ref[0, n - 1, :, qsl], qt_ref[0, qh, qsl, :],
                        preferred_element_type=jnp.float32)
            s_sc[slot, ln] = s
            smax_sc[slot, ln] = jnp.max(s, axis=0, keepdims=True)

    def softmax(slot, modes):
        if DIAG in modes:
            key_chunk = lax.broadcasted_iota(jnp.int32, (T, T), 0) // CHUNK
            qry_chunk = lax.broadcasted_iota(jnp.int32, (T, T), 1) // CHUNK
            visible = key_chunk <= qry_chunk
        for (_, ln, _, _, _), mode in zip(lanes, modes):
            if mode == SKIP:
                continue
            s = s_sc[slot, ln]
            if mode == DIAG:
                s = jnp.where(visible, s, NEG)
                tile_max = jnp.max(s, axis=0, keepdims=True)
            else:
                tile_max = smax_sc[slot, ln]
            m = m_sc[ln]
            m_new = jnp.maximum(m, tile_max)
            alpha_sc[slot, ln] = jnp.exp2(m - m_new)
            p_sc[slot, ln] = jnp.exp2(s - m_new).astype(jnp.bfloat16)
            m_sc[ln] = m_new

    def values(n, slot, modes):
        for (_, ln, _, vsl, asl), mode in zip(lanes, modes):
            if mode == SKIP:
                continue
            acc_sc[asl, :] = alpha_sc[slot, ln] * acc_sc[asl, :] + jnp.dot(
                _with_ones(vt_ref[0, jnp.maximum(n - 1, 0), vsl, :]), p_sc[slot, ln],
                preferred_element_type=jnp.float32)

    def stage(n, slot, modes, prev_modes, next_modes):
        for qsel in range(Q_TILES):
            sel = lambda ms: [m if qh == qsel else SKIP for (qh, *_), m in zip(lanes, ms)]
            softmax(slot, sel(modes))
            if next_modes is not None:
                scores(n + 1, 1 - slot, sel(next_modes))
            values(n - 1, 1 - slot, sel(prev_modes))

    p_sc[0] = jnp.zeros((len(lanes), T, T), jnp.bfloat16)
    alpha_sc[0] = jnp.ones((len(lanes), 1, T), jnp.float32)
    scores(1, 1, all_full)

    def pair(kk, carry):
        n = 2 * kk + 1
        stage(n, 1, all_full, all_full, all_full)
        stage(n + 1, 0, all_full, all_full, all_full)
        return carry

    lax.fori_loop(0, i, pair, 0)
    n_a = 2 * i + 1
    stage(n_a, 1, tile_a, all_full, tile_b)
    stage(n_a + 1, 0, tile_b, tile_a, None)
    values(n_a + 1, 0, tile_b)

    for qh in range(Q_TILES):
        o_ref[0, qh * T:(qh + 1) * T, :] = _attn_output(
            [acc_sc[asl, :] for q, _, _, _, asl in lanes if q == qh], g_ref[...])


def _ffn_kernel(x_ref, uph_ref, u_ref, o_ref, w_out_ref, ffn_g_ref, w_up_ref, fcw_ref,
                fcb_ref, w_down_ref, fin_g_ref, out_ref, upbuf, acc_ref):
    t = pl.program_id(1)
    h1 = (x_ref[0]
          + jnp.dot(u_ref[0], w_out_ref[0:D_CONV, :], preferred_element_type=jnp.float32)
          + jnp.dot(o_ref[0], w_out_ref[D_CONV:, :], preferred_element_type=jnp.float32))
    acc_ref[...] = h1
    n2 = _rms(h1, ffn_g_ref[...]).astype(jnp.bfloat16)

    @pl.when(t == 0)
    def _():
        upbuf[:, 0:SUBLANES, :] = uph_ref[...]

    def up_proj(c):
        return tuple(
            jnp.dot(n2, w_up_ref[:, col0:col0 + FF_CHUNK], preferred_element_type=jnp.float32)
            for col0 in (c * FF_CHUNK, D_FF + c * FF_CHUNK))

    def conv3(up, col0):
        outs = []
        for s in range(FF_CHUNK // LANES):
            slab = col0 // LANES + s
            cols = slice(col0 + s * LANES, col0 + (s + 1) * LANES)
            upbuf[slab, SUBLANES:SUBLANES + T_FFN, :] = up[:, s * LANES:(s + 1) * LANES]
            y = fcb_ref[:, cols] + jnp.zeros((T_FFN, LANES), jnp.float32)
            for k in range(FFN_CONV_WIDTH):
                off = SUBLANES - (FFN_CONV_WIDTH - 1) + k
                y = y + fcw_ref[k:k + 1, cols] * upbuf[slab, off:off + T_FFN, :]
            outs.append(y)
            upbuf[slab, 0:SUBLANES, :] = upbuf[slab, T_FFN:T_FFN + SUBLANES, :]
        return jnp.concatenate(outs, axis=-1)

    ups = [up_proj(c) for c in range(FF_AHEAD)]
    for c in range(N_FF_CHUNKS):
        if c + FF_AHEAD < N_FF_CHUNKS:
            ups.append(up_proj(c + FF_AHEAD))
        up_g, up_val = ups.pop(0)
        g = conv3(up_g, c * FF_CHUNK)
        val = conv3(up_val, D_FF + c * FF_CHUNK)
        act = (g * jax.nn.sigmoid(g) * val).astype(jnp.bfloat16)
        contrib = jnp.dot(act, w_down_ref[c * FF_CHUNK:(c + 1) * FF_CHUNK, :],
                          preferred_element_type=jnp.float32)
        acc_ref[...] += contrib
    out_ref[0] = _rms(acc_ref[...], fin_g_ref[...]).astype(out_ref.dtype)


def _full(shape):
    return pl.BlockSpec(shape, lambda *_: (0,) * len(shape))


def _rope_tables(pos):
    inv_freq = 1.0 / (ROPE_THETA ** (np.arange(0, QK_ROPE, 2, dtype=np.float64) / QK_ROPE))
    ang = np.asarray(pos, np.float64)[:, None] * inv_freq[None, :]
    cos, sin = np.cos(ang), np.sin(ang)
    zl = lambda n: np.zeros((len(pos), n))
    kc = np.concatenate([zl(QK_NOPE), cos, cos, zl(LANES - QK_NOPE - QK_ROPE)], axis=1)
    ks1 = np.concatenate([zl(QK_NOPE + ROPE_HALF), sin, zl(LANES - QK_NOPE - QK_ROPE)], axis=1)
    ks2 = np.concatenate([zl(QK_NOPE), -sin, zl(LANES - QK_NOPE - ROPE_HALF)], axis=1)
    tabs = (kc, ks1, ks2, (cos * Q_SCALE).T, (sin * Q_SCALE).T)
    return tuple(jnp.asarray(t.astype(np.float32)) for t in tabs)


def kernel(x, meta_tokens, mix_norm_g, w_in, q_norm_g, w_uq, kv_norm_g, w_ukv, conv_w, conv_b,
           conv_ln_g, conv_ln_b, conv_out_g, attn_out_g, w_out, ffn_norm_g, w_ffn_up,
           ffn_conv_w, ffn_conv_b, w_ffn_down, final_norm_g):
    B, S, D = x.shape
    assert D == D_MODEL and S % (Q_TILES * T) == 0 and S % T_FFN == 0 and mix_norm_g.shape[0] == 1
    assert meta_tokens.shape == (N_META, D)
    nt = S // T
    bf16, f32 = jnp.bfloat16, jnp.float32
    row2 = lambda v: v.reshape(1, -1).astype(f32)

    w_in0 = w_in[0]
    glu_cols = w_in0[:, :2 * D_CONV].reshape(D, 2, N_CONV_SLABS, LANES).transpose(
        0, 2, 1, 3).reshape(D, 2 * D_CONV)
    w_in_p = jnp.concatenate([
        glu_cols, w_in0[:, 2 * D_CONV:KR_OFF],
        jnp.zeros((D, QK_NOPE), f32), w_in0[:, KR_OFF:], jnp.zeros((D, LANES - QK_NOPE - QK_ROPE), f32),
    ], axis=1).astype(bf16)
    w_uq3 = w_uq[0].reshape(Q_LORA, N_HEADS, QK_NOPE + QK_ROPE)
    w_uqt = jnp.pad(w_uq3, ((0, 0), (0, 0), (0, HEAD_PAD - QK_NOPE - QK_ROPE))).reshape(
        Q_LORA, N_HEADS * HEAD_PAD).T.astype(bf16)
    w_ukv3 = w_ukv[0].reshape(KV_LORA, N_HEADS, QK_NOPE + V_HEAD)
    w_uk = jnp.pad(w_ukv3[:, :, :QK_NOPE], ((0, 0), (0, 0), (0, HEAD_PAD - QK_NOPE))).reshape(
        KV_LORA, N_HEADS * HEAD_PAD).astype(bf16)
    w_uvt = w_ukv3[:, :, QK_NOPE:].reshape(KV_LORA, D_ATTN).T.astype(bf16)
    w_out_b = w_out[0].astype(bf16)
    w_up_b = w_ffn_up[0].astype(bf16)
    w_down_b = w_ffn_down[0].astype(bf16)
    mix_g, q_g, kv_g = row2(mix_norm_g[0]), row2(q_norm_g[0]), row2(kv_norm_g[0])
    cw, cb = conv_w[0].astype(f32), row2(conv_b[0])
    ln_g, ln_b, cog = row2(conv_ln_g[0]), row2(conv_ln_b[0]), row2(conv_out_g[0])
    attn_g, ffn_g = row2(attn_out_g[0]), row2(ffn_norm_g[0])

    meta_tabs = _rope_tables(np.arange(META_ROWS))
    tabs = _rope_tables(np.arange(S) + N_META)

    proj_w = [mix_g, w_in_p, q_g, w_uqt, kv_g, w_uk, w_uvt]
    proj_specs = [_full((1, D)), _full((D, D_IN_PAD)), _full((1, Q_LORA)),
                  _full((N_HEADS * HEAD_PAD, Q_LORA)), _full((1, KV_LORA)),
                  _full((KV_LORA, N_HEADS * HEAD_PAD)), _full((D_ATTN, KV_LORA))]
    conv_w_ = [cw, cb, ln_g, ln_b, cog]
    conv_specs = [_full((CONV_WIDTH, D_CONV)), _full((1, D_CONV)), _full((1, D_CONV)),
                  _full((1, D_CONV)), _full((1, D_CONV))]
    front_w, front_specs = proj_w + conv_w_, proj_specs + conv_specs
    seq = ("arbitrary", "arbitrary")

    conv_halo, k_meta, v_meta, up_halo = pl.pallas_call(
        _meta_kernel,
        grid=(1,),
        in_specs=[_full((N_META, D))] + front_specs + [
            _full((META_ROWS, LANES)), _full((META_ROWS, LANES)), _full((META_ROWS, LANES)),
            _full((ROPE_HALF, META_ROWS)), _full((ROPE_HALF, META_ROWS)),
            _full((1, D_ATTN)), _full((D_CONV + D_ATTN, D)), _full((1, D)), _full((D, 2 * D_FF))],
        out_specs=[_full((N_CONV_SLABS, HALO, LANES)), _full((N_META, N_HEADS * HEAD_PAD)),
                   _full((D_ATTN, META_ROWS)), _full((N_FF_SLABS, SUBLANES, LANES))],
        out_shape=[jax.ShapeDtypeStruct((N_CONV_SLABS, HALO, LANES), f32),
                   jax.ShapeDtypeStruct((N_META, N_HEADS * HEAD_PAD), bf16),
                   jax.ShapeDtypeStruct((D_ATTN, META_ROWS), bf16),
                   jax.ShapeDtypeStruct((N_FF_SLABS, SUBLANES, LANES), f32)],
        scratch_shapes=[pltpu.VMEM((N_CONV_SLABS, HALO + META_ROWS, LANES), f32)],
        compiler_params=pltpu.CompilerParams(dimension_semantics=("arbitrary",),
                                             vmem_limit_bytes=VMEM_LIMIT),
        name="meta_tokens",
    )(meta_tokens.astype(f32), *front_w, *meta_tabs, attn_g, w_out_b, ffn_g, w_up_b)

    params = pltpu.CompilerParams(dimension_semantics=seq, vmem_limit_bytes=VMEM_LIMIT)
    u_n, k4, qt4, vt4 = pl.pallas_call(
        _proj_kernel,
        grid=(B, S // T_PROJ),
        in_specs=[pl.BlockSpec((1, T_PROJ, D), lambda b, t: (b, t, 0)),
                  _full((N_CONV_SLABS, HALO, LANES))] + front_specs + [
            pl.BlockSpec((T_PROJ, LANES), lambda b, t: (t, 0)),
            pl.BlockSpec((T_PROJ, LANES), lambda b, t: (t, 0)),
            pl.BlockSpec((T_PROJ, LANES), lambda b, t: (t, 0)),
            pl.BlockSpec((ROPE_HALF, T_PROJ), lambda b, t: (0, t)),
            pl.BlockSpec((ROPE_HALF, T_PROJ), lambda b, t: (0, t))],
        out_specs=[
            pl.BlockSpec((1, T_PROJ, D_CONV), lambda b, t: (b, t, 0)),
            pl.BlockSpec((1, T_PROJ // T, T, N_HEADS * HEAD_PAD), lambda b, t: (b, t, 0, 0)),
            pl.BlockSpec((1, T_PROJ // T, N_HEADS * HEAD_PAD, T), lambda b, t: (b, t, 0, 0)),
            pl.BlockSpec((1, T_PROJ // T, D_ATTN, T), lambda b, t: (b, t, 0, 0)),
        ],
        out_shape=[
            jax.ShapeDtypeStruct((B, S, D_CONV), bf16),
            jax.ShapeDtypeStruct((B, nt, T, N_HEADS * HEAD_PAD), bf16),
            jax.ShapeDtypeStruct((B, nt, N_HEADS * HEAD_PAD, T), bf16),
            jax.ShapeDtypeStruct((B, nt, D_ATTN, T), bf16),
        ],
        scratch_shapes=[pltpu.VMEM((N_CONV_SLABS, HALO + T_PROJ, LANES), f32)],
        compiler_params=params,
        name="proj_conv_qkv",
    )(x, conv_halo, *front_w, *tabs)

    n_lanes = Q_TILES * N_HEADS
    o_n = pl.pallas_call(
        _attn_kernel,
        grid=(B, nt // Q_TILES),
        in_specs=[
            pl.BlockSpec((1, Q_TILES, N_HEADS * HEAD_PAD, T), lambda b, i: (b, i, 0, 0)),
            pl.BlockSpec((1, nt, T, N_HEADS * HEAD_PAD), lambda b, i: (b, 0, 0, 0)),
            pl.BlockSpec((1, nt, D_ATTN, T), lambda b, i: (b, 0, 0, 0)),
            _full((N_META, N_HEADS * HEAD_PAD)), _full((D_ATTN, META_ROWS)), _full((1, D_ATTN)),
        ],
        out_specs=pl.BlockSpec((1, Q_TILES * T, D_ATTN), lambda b, i: (b, i, 0)),
        out_shape=jax.ShapeDtypeStruct((B, S, D_ATTN), bf16),
        scratch_shapes=[pltpu.VMEM((n_lanes, 1, T), f32),
                        pltpu.VMEM((n_lanes * V_AUG, T), f32),
                        pltpu.VMEM((2, n_lanes, 1, T), f32),
                        pltpu.VMEM((2, n_lanes, 1, T), f32),
                        pltpu.VMEM((2, n_lanes, T, T), f32),
                        pltpu.VMEM((2, n_lanes, T, T), bf16)],
        compiler_params=params,
        name="block_causal_attn",
    )(qt4, k4, vt4, k_meta, v_meta, attn_g)

    nf = S // T_FFN
    out = pl.pallas_call(
        _ffn_kernel,
        grid=(B, nf),
        in_specs=[
            pl.BlockSpec((1, T_FFN, D), lambda b, t: (b, t, 0)),
            _full((N_FF_SLABS, SUBLANES, LANES)),
            pl.BlockSpec((1, T_FFN, D_CONV), lambda b, t: (b, t, 0)),
            pl.BlockSpec((1, T_FFN, D_ATTN), lambda b, t: (b, t, 0)),
            _full((D_CONV + D_ATTN, D)), _full((1, D)), _full((D, 2 * D_FF)),
            _full((FFN_CONV_WIDTH, 2 * D_FF)), _full((1, 2 * D_FF)), _full((D_FF, D)),
            _full((1, D)),
        ],
        out_specs=pl.BlockSpec((1, T_FFN, D), lambda b, t: (b, t, 0)),
        out_shape=jax.ShapeDtypeStruct((B, S, D), x.dtype),
        scratch_shapes=[pltpu.VMEM((N_FF_SLABS, SUBLANES + T_FFN, LANES), f32),
                        pltpu.VMEM((T_FFN, D), f32)],
        compiler_params=params,
        name="outproj_convffn",
    )(x, up_halo, u_n, o_n, w_out_b, ffn_g, w_up_b, ffn_conv_w[0].astype(f32),
      row2(ffn_conv_b[0]), w_down_b, row2(final_norm_g))
    return out
```

```python
import jax
import jax.numpy as jnp
import numpy as np
from jax import lax
from jax.experimental import pallas as pl
from jax.experimental.pallas import tpu as pltpu

D_MODEL = 1024
CHUNK = 64
N_META = 16
D_CONV = 512
CONV_WIDTH = 31
N_HEADS = 8
QK_NOPE = 64
QK_ROPE = 32
V_HEAD = 64
D_ATTN = N_HEADS * V_HEAD
Q_LORA = 384
KV_LORA = 256
ROPE_THETA = 10000.0
D_FF = 2816
FFN_CONV_WIDTH = 3
EPS = 1e-6
NEG = -1e30

LANES = 128
SUBLANES = 8
T = 256
T_PROJ = 512
T_FFN = 512
Q_TILES = 2
META_ROWS = 128
HEAD_PAD = 128
V_AUG = V_HEAD + 16
HALO = 32
CONV_ROWS = 64
ROPE_HALF = QK_ROPE // 2
D_IN_PAD = 2 * D_CONV + Q_LORA + KV_LORA + LANES
KR_OFF = 2 * D_CONV + Q_LORA + KV_LORA
N_CONV_SLABS = D_CONV // LANES
N_FF_SLABS = 2 * D_FF // LANES
FF_CHUNK = 256
N_FF_CHUNKS = D_FF // FF_CHUNK
FF_AHEAD = 3
VMEM_LIMIT = 56 * 1024 * 1024
Q_SCALE = (QK_NOPE + QK_ROPE) ** -0.5 * 1.4426950408889634
NT_DIMS = (((1,), (1,)), ((), ()))


def _rms(x, g):
    return x * lax.rsqrt(jnp.mean(x * x, axis=-1, keepdims=True) + EPS) * g


def _head_slices():
    return [(slice(hd * HEAD_PAD, (hd + 1) * HEAD_PAD), slice(hd * V_HEAD, (hd + 1) * V_HEAD),
             slice(hd * V_AUG, (hd + 1) * V_AUG)) for hd in range(N_HEADS)]


def _with_ones(vt):
    return jnp.concatenate([vt, jnp.ones((V_AUG - V_HEAD, vt.shape[1]), jnp.bfloat16)], axis=0)


def _conv_group(n, w_in_ref, xbuf, cw_ref, cb_ref, ln_g_ref, ln_b_ref, cog_ref):
    rows = n.shape[0]
    block = min(CONV_ROWS, rows)
    conv_slabs = []
    for s in range(N_CONV_SLABS):
        lanes = slice(s * LANES, (s + 1) * LANES)
        w_slab = jnp.concatenate([w_in_ref[:, lanes],
                                  w_in_ref[:, D_CONV + s * LANES:D_CONV + (s + 1) * LANES]], axis=1)
        z = jnp.dot(n, w_slab, preferred_element_type=jnp.float32)
        xbuf[s, HALO:HALO + rows, :] = z[:, :LANES] * jax.nn.sigmoid(z[:, LANES:])
        blocks = []
        for r0 in range(0, rows, block):
            acc = jnp.zeros((block, LANES), jnp.float32) + cb_ref[:, lanes]
            for k in range(CONV_WIDTH):
                off = HALO - (CONV_WIDTH - 1) + k + r0
                acc = acc + cw_ref[k:k + 1, lanes] * xbuf[s, off:off + block, :]
            blocks.append(acc)
        conv_slabs.append(jnp.concatenate(blocks, axis=0))
        xbuf[s, 0:HALO, :] = xbuf[s, rows:rows + HALO, :]
    c = jnp.concatenate(conv_slabs, axis=-1)
    mu = jnp.mean(c, axis=-1, keepdims=True)
    var = jnp.mean(jnp.square(c - mu), axis=-1, keepdims=True)
    c = (c - mu) * lax.rsqrt(var + EPS) * ln_g_ref[...] + ln_b_ref[...]
    c = c * jax.nn.sigmoid(c)
    return _rms(c, cog_ref[...]).astype(jnp.bfloat16)


def _qkv(n, w_in_ref, q_g_ref, w_uqt_ref, kv_g_ref, w_uk_ref, w_uvt_ref, kc, ks1, ks2, qcos, qsin):
    c_q = jnp.dot(n, w_in_ref[:, 2 * D_CONV:2 * D_CONV + Q_LORA],
                  preferred_element_type=jnp.float32)
    z_kv = jnp.dot(n, w_in_ref[:, 2 * D_CONV + Q_LORA:D_IN_PAD], preferred_element_type=jnp.float32)
    c_kv = z_kv[:, :KV_LORA]
    k_r = pltpu.roll(z_kv[:, KV_LORA:], QK_NOPE, 1)
    qn = _rms(c_q, q_g_ref[...]).astype(jnp.bfloat16)
    kvn = _rms(c_kv, kv_g_ref[...]).astype(jnp.bfloat16)

    k_rot = (k_r * kc + pltpu.roll(k_r, ROPE_HALF, 1) * ks1
             + pltpu.roll(k_r, LANES - ROPE_HALF, 1) * ks2)
    k_nope = jnp.dot(kvn, w_uk_ref[...], preferred_element_type=jnp.float32)
    k = jnp.concatenate(
        [(k_nope[:, hd * HEAD_PAD:(hd + 1) * HEAD_PAD] + k_rot).astype(jnp.bfloat16)
         for hd in range(N_HEADS)], axis=-1)

    vt = lax.dot_general(w_uvt_ref[...], kvn, NT_DIMS,
                         preferred_element_type=jnp.float32).astype(jnp.bfloat16)
    qt = lax.dot_general(w_uqt_ref[...], qn, NT_DIMS, preferred_element_type=jnp.float32)
    pieces = []
    for hd in range(N_HEADS):
        b0 = hd * HEAD_PAD
        x1 = qt[b0 + QK_NOPE:b0 + QK_NOPE + ROPE_HALF]
        x2 = qt[b0 + QK_NOPE + ROPE_HALF:b0 + QK_NOPE + QK_ROPE]
        pieces += [qt[b0:b0 + QK_NOPE] * Q_SCALE, x1 * qcos - x2 * qsin, x2 * qcos + x1 * qsin,
                   jnp.zeros((HEAD_PAD - QK_NOPE - QK_ROPE, qt.shape[1]), jnp.float32)]
    return k, jnp.concatenate(pieces, axis=0).astype(jnp.bfloat16), vt


def _meta_keys_softmax(km, vm, qt):
    heads = _head_slices()
    nq = qt.shape[1]
    scores = [jnp.dot(km[:, qsl], qt[qsl, :], preferred_element_type=jnp.float32)
              for qsl, _, _ in heads]
    maxes, probs = [], []
    for s in scores:
        m = jnp.max(s, axis=0, keepdims=True)
        maxes.append(m)
        probs.append(jnp.concatenate(
            [jnp.exp2(s - m).astype(jnp.bfloat16),
             jnp.zeros((LANES - N_META, nq), jnp.bfloat16)], axis=0))
    accs = [jnp.dot(_with_ones(vm[vsl, :]), p, preferred_element_type=jnp.float32)
            for (_, vsl, _), p in zip(heads, probs)]
    return maxes, accs


def _attn_output(acc_of_head, g):
    ot = jnp.concatenate([a[:V_HEAD] / a[V_HEAD:V_HEAD + 1] for a in acc_of_head], axis=0)
    ot = ot * lax.rsqrt(jnp.mean(ot * ot, axis=0, keepdims=True) + EPS)
    return (ot.T * g).astype(jnp.bfloat16)


def _meta_kernel(meta_ref, mix_g_ref, w_in_ref, q_g_ref, w_uqt_ref, kv_g_ref, w_uk_ref, w_uvt_ref,
                 cw_ref, cb_ref, ln_g_ref, ln_b_ref, cog_ref, kc_ref, ks1_ref, ks2_ref,
                 qcos_ref, qsin_ref, attn_g_ref, w_out_ref, ffn_g_ref, w_up_ref,
                 halo_ref, km_ref, vm_ref, uph_ref, xbuf):
    h = jnp.concatenate(
        [meta_ref[...], jnp.zeros((META_ROWS - N_META, D_MODEL), jnp.float32)], axis=0)
    n = _rms(h, mix_g_ref[...]).astype(jnp.bfloat16)

    xbuf[:, 0:HALO, :] = jnp.zeros((N_CONV_SLABS, HALO, LANES), jnp.float32)
    u_n = _conv_group(n, w_in_ref, xbuf, cw_ref, cb_ref, ln_g_ref, ln_b_ref, cog_ref)
    halo_ref[:, 0:HALO - N_META, :] = jnp.zeros((N_CONV_SLABS, HALO - N_META, LANES), jnp.float32)
    halo_ref[:, HALO - N_META:HALO, :] = xbuf[:, HALO:HALO + N_META, :]

    k, qt, vt = _qkv(n, w_in_ref, q_g_ref, w_uqt_ref, kv_g_ref, w_uk_ref, w_uvt_ref,
                     kc_ref[...], ks1_ref[...], ks2_ref[...], qcos_ref[...], qsin_ref[...])
    km = k[0:N_META]
    key_is_meta = lax.broadcasted_iota(jnp.int32, (D_ATTN, META_ROWS), 1) < N_META
    vm = jnp.where(key_is_meta, vt, jnp.zeros_like(vt))
    km_ref[...] = km
    vm_ref[...] = vm

    _, accs = _meta_keys_softmax(km, vm, qt)
    o_n = _attn_output(accs, attn_g_ref[...])

    h1 = (h + jnp.dot(u_n, w_out_ref[0:D_CONV, :], preferred_element_type=jnp.float32)
          + jnp.dot(o_n, w_out_ref[D_CONV:, :], preferred_element_type=jnp.float32))
    n2 = _rms(h1, ffn_g_ref[...]).astype(jnp.bfloat16)
    for c in range(2 * D_FF // FF_CHUNK):
        up = jnp.dot(n2, w_up_ref[:, c * FF_CHUNK:(c + 1) * FF_CHUNK],
                     preferred_element_type=jnp.float32)
        for s in range(FF_CHUNK // LANES):
            uph_ref[c * (FF_CHUNK // LANES) + s] = up[N_META - SUBLANES:N_META,
                                                      s * LANES:(s + 1) * LANES]


def _proj_kernel(x_ref, halo_ref, mix_g_ref, w_in_ref, q_g_ref, w_uqt_ref, kv_g_ref,
                 w_uk_ref, w_uvt_ref, cw_ref, cb_ref, ln_g_ref, ln_b_ref, cog_ref,
                 kc_ref, ks1_ref, ks2_ref, qcos_ref, qsin_ref,
                 u_ref, k_ref, qt_ref, vt_ref, xbuf):
    @pl.when(pl.program_id(1) == 0)
    def _():
        xbuf[:, 0:HALO, :] = halo_ref[...]

    n = _rms(x_ref[0], mix_g_ref[...]).astype(jnp.bfloat16)
    u_ref[0] = _conv_group(n, w_in_ref, xbuf, cw_ref, cb_ref, ln_g_ref, ln_b_ref, cog_ref)
    k, qt, vt = _qkv(n, w_in_ref, q_g_ref, w_uqt_ref, kv_g_ref, w_uk_ref, w_uvt_ref,
                     kc_ref[...], ks1_ref[...], ks2_ref[...], qcos_ref[...], qsin_ref[...])
    for j in range(T_PROJ // T):
        k_ref[0, j] = k[j * T:(j + 1) * T]
        qt_ref[0, j] = qt[:, j * T:(j + 1) * T]
        vt_ref[0, j] = vt[:, j * T:(j + 1) * T]


def _attn_kernel(qt_ref, k_ref, vt_ref, km_ref, vm_ref, g_ref, o_ref,
                 m_sc, acc_sc, alpha_sc, smax_sc, s_sc, p_sc):
    i = pl.program_id(1)
    heads = _head_slices()
    lanes = [(qh, qh * N_HEADS + hd, qsl, vsl, slice((qh * N_HEADS + hd) * V_AUG,
                                                   (qh * N_HEADS + hd + 1) * V_AUG))
             for qh in range(Q_TILES) for hd, (qsl, vsl, _) in enumerate(heads)]
    FULL, DIAG, SKIP = "full", "diag", "skip"
    all_full = [FULL] * len(lanes)
    tile_a = [DIAG if qh == 0 else FULL for qh, *_ in lanes]
    tile_b = [SKIP if qh == 0 else DIAG for qh, *_ in lanes]

    for qh in range(Q_TILES):
        maxes, accs = _meta_keys_softmax(km_ref[...], vm_ref[...], qt_ref[0, qh])
        for hd in range(N_HEADS):
            _, ln, _, _, asl = lanes[qh * N_HEADS + hd]
            m_sc[ln] = maxes[hd]
            acc_sc[asl, :] = accs[hd]

    def scores(n, slot, modes):
        for (qh, ln, qsl, _, _), mode in zip(lanes, modes):
            if mode == SKIP:
                continue
            s = jnp.dot(k_ref[0, n - 1, :, qsl], qt_ref[0, qh, qsl, :],
                        preferred_element_type=jnp.float32)
            s_sc[slot, ln] = s
            smax_sc[slot, ln] = jnp.max(s, axis=0, keepdims=True)

    def softmax(slot, modes):
        if DIAG in modes:
            key_chunk = lax.broadcasted_iota(jnp.int32, (T, T), 0) // CHUNK
            qry_chunk = lax.broadcasted_iota(jnp.int32, (T, T), 1) // CHUNK
            visible = key_chunk <= qry_chunk
        for (_, ln, _, _, _), mode in zip(lanes, modes):
            if mode == SKIP:
                continue
            s = s_sc[slot, ln]
            if mode == DIAG:
                s = jnp.where(visible, s, NEG)
                tile_max = jnp.max(s, axis=0, keepdims=True)
            else:
                tile_max = smax_sc[slot, ln]
            m = m_sc[ln]
            m_new = jnp.maximum(m, tile_max)
            alpha_sc[slot, ln] = jnp.exp2(m - m_new)
            p_sc[slot, ln] = jnp.exp2(s - m_new).astype(jnp.bfloat16)
            m_sc[ln] = m_new

    def values(n, slot, modes):
        for (_, ln, _, vsl, asl), mode in zip(lanes, modes):
            if mode == SKIP:
                continue
            acc_sc[asl, :] = alpha_sc[slot, ln] * acc_sc[asl, :] + jnp.dot(
                _with_ones(vt_ref[0, jnp.maximum(n - 1, 0), vsl, :]), p_sc[slot, ln],
                preferred_element_type=jnp.float32)

    def stage(n, slot, modes, prev_modes, next_modes):
        for qsel in range(Q_TILES):
            sel = lambda ms: [m if qh == qsel else SKIP for (qh, *_), m in zip(lanes, ms)]
            softmax(slot, sel(modes))
            if next_modes is not None:
                scores(n + 1, 1 - slot, sel(next_modes))
            values(n - 1, 1 - slot, sel(prev_modes))

    p_sc[0] = jnp.zeros((len(lanes), T, T), jnp.bfloat16)
    alpha_sc[0] = jnp.ones((len(lanes), 1, T), jnp.float32)
    scores(1, 1, all_full)

    def pair(kk, carry):
        n = 2 * kk + 1
        stage(n, 1, all_full, all_full, all_full)
        stage(n + 1, 0, all_full, all_full, all_full)
        return carry

    lax.fori_loop(0, i, pair, 0)
    n_a = 2 * i + 1
    stage(n_a, 1, tile_a, all_full, tile_b)
    stage(n_a + 1, 0, tile_b, tile_a, None)
    values(n_a + 1, 0, tile_b)

    for qh in range(Q_TILES):
        o_ref[0, qh * T:(qh + 1) * T, :] = _attn_output(
            [acc_sc[asl, :] for q, _, _, _, asl in lanes if q == qh], g_ref[...])


def _ffn_kernel(x_ref, uph_ref, u_ref, o_ref, w_out_ref, ffn_g_ref, w_up_ref, fcw_ref,
                fcb_ref, w_down_ref, fin_g_ref, out_ref, upbuf, acc_ref):
    t = pl.program_id(1)
    h1 = (x_ref[0]
          + jnp.dot(u_ref[0], w_out_ref[0:D_CONV, :], preferred_element_type=jnp.float32)
          + jnp.dot(o_ref[0], w_out_ref[D_CONV:, :], preferred_element_type=jnp.float32))
    acc_ref[...] = h1
    n2 = _rms(h1, ffn_g_ref[...]).astype(jnp.bfloat16)

    @pl.when(t == 0)
    def _():
        upbuf[:, 0:SUBLANES, :] = uph_ref[...]

    def up_proj(c):
        return tuple(
            jnp.dot(n2, w_up_ref[:, col0:col0 + FF_CHUNK], preferred_element_type=jnp.float32)
            for col0 in (c * FF_CHUNK, D_FF + c * FF_CHUNK))

    def conv3(up, col0):
        outs = []
        for s in range(FF_CHUNK // LANES):
            slab = col0 // LANES + s
            cols = slice(col0 + s * LANES, col0 + (s + 1) * LANES)
            upbuf[slab, SUBLANES:SUBLANES + T_FFN, :] = up[:, s * LANES:(s + 1) * LANES]
            y = fcb_ref[:, cols] + jnp.zeros((T_FFN, LANES), jnp.float32)
            for k in range(FFN_CONV_WIDTH):
                off = SUBLANES - (FFN_CONV_WIDTH - 1) + k
                y = y + fcw_ref[k:k + 1, cols] * upbuf[slab, off:off + T_FFN, :]
            outs.append(y)
            upbuf[slab, 0:SUBLANES, :] = upbuf[slab, T_FFN:T_FFN + SUBLANES, :]
        return jnp.concatenate(outs, axis=-1)

    ups = [up_proj(c) for c in range(FF_AHEAD)]
    for c in range(N_FF_CHUNKS):
        if c + FF_AHEAD < N_FF_CHUNKS:
            ups.append(up_proj(c + FF_AHEAD))
        up_g, up_val = ups.pop(0)
        g = conv3(up_g, c * FF_CHUNK)
        val = conv3(up_val, D_FF + c * FF_CHUNK)
        act = (g * jax.nn.sigmoid(g) * val).astype(jnp.bfloat16)
        contrib = jnp.dot(act, w_down_ref[c * FF_CHUNK:(c + 1) * FF_CHUNK, :],
                          preferred_element_type=jnp.float32)
        acc_ref[...] += contrib
    out_ref[0] = _rms(acc_ref[...], fin_g_ref[...]).astype(out_ref.dtype)


def _full(shape):
    return pl.BlockSpec(shape, lambda *_: (0,) * len(shape))


def _rope_tables(pos):
    inv_freq = 1.0 / (ROPE_THETA ** (np.arange(0, QK_ROPE, 2, dtype=np.float64) / QK_ROPE))
    ang = np.asarray(pos, np.float64)[:, None] * inv_freq[None, :]
    cos, sin = np.cos(ang), np.sin(ang)
    zl = lambda n: np.zeros((len(pos), n))
    kc = np.concatenate([zl(QK_NOPE), cos, cos, zl(LANES - QK_NOPE - QK_ROPE)], axis=1)
    ks1 = np.concatenate([zl(QK_NOPE + ROPE_HALF), sin, zl(LANES - QK_NOPE - QK_ROPE)], axis=1)
    ks2 = np.concatenate([zl(QK_NOPE), -sin, zl(LANES - QK_NOPE - ROPE_HALF)], axis=1)
    tabs = (kc, ks1, ks2, (cos * Q_SCALE).T, (sin * Q_SCALE).T)
    return tuple(jnp.asarray(t.astype(np.float32)) for t in tabs)


def kernel(x, meta_tokens, mix_norm_g, w_in, q_norm_g, w_uq, kv_norm_g, w_ukv, conv_w, conv_b,
           conv_ln_g, conv_ln_b, conv_out_g, attn_out_g, w_out, ffn_norm_g, w_ffn_up,
           ffn_conv_w, ffn_conv_b, w_ffn_down, final_norm_g):
    B, S, D = x.shape
    assert D == D_MODEL and S % (Q_TILES * T) == 0 and S % T_FFN == 0 and mix_norm_g.shape[0] == 1
    assert meta_tokens.shape == (N_META, D)
    nt = S // T
    bf16, f32 = jnp.bfloat16, jnp.float32
    row2 = lambda v: v.reshape(1, -1).astype(f32)

    w_in_p = jnp.pad(w_in[0].astype(bf16), ((0, 0), (0, LANES - QK_ROPE)))
    w_uq3 = w_uq[0].reshape(Q_LORA, N_HEADS, QK_NOPE + QK_ROPE)
    w_uqt = jnp.pad(w_uq3, ((0, 0), (0, 0), (0, HEAD_PAD - QK_NOPE - QK_ROPE))).reshape(
        Q_LORA, N_HEADS * HEAD_PAD).T.astype(bf16)
    w_ukv3 = w_ukv[0].reshape(KV_LORA, N_HEADS, QK_NOPE + V_HEAD)
    w_uk = jnp.pad(w_ukv3[:, :, :QK_NOPE], ((0, 0), (0, 0), (0, HEAD_PAD - QK_NOPE))).reshape(
        KV_LORA, N_HEADS * HEAD_PAD).astype(bf16)
    w_uvt = w_ukv3[:, :, QK_NOPE:].reshape(KV_LORA, D_ATTN).T.astype(bf16)
    w_out_b = w_out[0].astype(bf16)
    w_up_b = w_ffn_up[0].astype(bf16)
    w_down_b = w_ffn_down[0].astype(bf16)
    mix_g, q_g, kv_g = row2(mix_norm_g[0]), row2(q_norm_g[0]), row2(kv_norm_g[0])
    cw, cb = conv_w[0].astype(f32), row2(conv_b[0])
    ln_g, ln_b, cog = row2(conv_ln_g[0]), row2(conv_ln_b[0]), row2(conv_out_g[0])
    attn_g, ffn_g = row2(attn_out_g[0]), row2(ffn_norm_g[0])

    meta_tabs = _rope_tables(np.arange(META_ROWS))
    tabs = _rope_tables(np.arange(S) + N_META)

    proj_w = [mix_g, w_in_p, q_g, w_uqt, kv_g, w_uk, w_uvt]
    proj_specs = [_full((1, D)), _full((D, D_IN_PAD)), _full((1, Q_LORA)),
                  _full((N_HEADS * HEAD_PAD, Q_LORA)), _full((1, KV_LORA)),
                  _full((KV_LORA, N_HEADS * HEAD_PAD)), _full((D_ATTN, KV_LORA))]
    conv_w_ = [cw, cb, ln_g, ln_b, cog]
    conv_specs = [_full((CONV_WIDTH, D_CONV)), _full((1, D_CONV)), _full((1, D_CONV)),
                  _full((1, D_CONV)), _full((1, D_CONV))]
    front_w, front_specs = proj_w + conv_w_, proj_specs + conv_specs
    seq = ("arbitrary", "arbitrary")

    conv_halo, k_meta, v_meta, up_halo = pl.pallas_call(
        _meta_kernel,
        grid=(1,),
        in_specs=[_full((N_META, D))] + front_specs + [
            _full((META_ROWS, LANES)), _full((META_ROWS, LANES)), _full((META_ROWS, LANES)),
            _full((ROPE_HALF, META_ROWS)), _full((ROPE_HALF, META_ROWS)),
            _full((1, D_ATTN)), _full((D_CONV + D_ATTN, D)), _full((1, D)), _full((D, 2 * D_FF))],
        out_specs=[_full((N_CONV_SLABS, HALO, LANES)), _full((N_META, N_HEADS * HEAD_PAD)),
                   _full((D_ATTN, META_ROWS)), _full((N_FF_SLABS, SUBLANES, LANES))],
        out_shape=[jax.ShapeDtypeStruct((N_CONV_SLABS, HALO, LANES), f32),
                   jax.ShapeDtypeStruct((N_META, N_HEADS * HEAD_PAD), bf16),
                   jax.ShapeDtypeStruct((D_ATTN, META_ROWS), bf16),
                   jax.ShapeDtypeStruct((N_FF_SLABS, SUBLANES, LANES), f32)],
        scratch_shapes=[pltpu.VMEM((N_CONV_SLABS, HALO + META_ROWS, LANES), f32)],
        compiler_params=pltpu.CompilerParams(dimension_semantics=("arbitrary",),
                                             vmem_limit_bytes=VMEM_LIMIT),
        name="meta_tokens",
    )(meta_tokens.astype(f32), *front_w, *meta_tabs, attn_g, w_out_b, ffn_g, w_up_b)

    params = pltpu.CompilerParams(dimension_semantics=seq, vmem_limit_bytes=VMEM_LIMIT)
    u_n, k4, qt4, vt4 = pl.pallas_call(
        _proj_kernel,
        grid=(B, S // T_PROJ),
        in_specs=[pl.BlockSpec((1, T_PROJ, D), lambda b, t: (b, t, 0)),
                  _full((N_CONV_SLABS, HALO, LANES))] + front_specs + [
            pl.BlockSpec((T_PROJ, LANES), lambda b, t: (t, 0)),
            pl.BlockSpec((T_PROJ, LANES), lambda b, t: (t, 0)),
            pl.BlockSpec((T_PROJ, LANES), lambda b, t: (t, 0)),
            pl.BlockSpec((ROPE_HALF, T_PROJ), lambda b, t: (0, t)),
            pl.BlockSpec((ROPE_HALF, T_PROJ), lambda b, t: (0, t))],
        out_specs=[
            pl.BlockSpec((1, T_PROJ, D_CONV), lambda b, t: (b, t, 0)),
            pl.BlockSpec((1, T_PROJ // T, T, N_HEADS * HEAD_PAD), lambda b, t: (b, t, 0, 0)),
            pl.BlockSpec((1, T_PROJ // T, N_HEADS * HEAD_PAD, T), lambda b, t: (b, t, 0, 0)),
            pl.BlockSpec((1, T_PROJ // T, D_ATTN, T), lambda b, t: (b, t, 0, 0)),
        ],
        out_shape=[
            jax.ShapeDtypeStruct((B, S, D_CONV), bf16),
            jax.ShapeDtypeStruct((B, nt, T, N_HEADS * HEAD_PAD), bf16),
            jax.ShapeDtypeStruct((B, nt, N_HEADS * HEAD_PAD, T), bf16),
            jax.ShapeDtypeStruct((B, nt, D_ATTN, T), bf16),
        ],
        scratch_shapes=[pltpu.VMEM((N_CONV_SLABS, HALO + T_PROJ, LANES), f32)],
        compiler_params=params,
        name="proj_conv_qkv",
    )(x, conv_halo, *front_w, *tabs)

    n_lanes = Q_TILES * N_HEADS
    o_n = pl.pallas_call(
        _attn_kernel,
        grid=(B, nt // Q_TILES),
        in_specs=[
            pl.BlockSpec((1, Q_TILES, N_HEADS * HEAD_PAD, T), lambda b, i: (b, i, 0, 0)),
            pl.BlockSpec((1, nt, T, N_HEADS * HEAD_PAD), lambda b, i: (b, 0, 0, 0)),
            pl.BlockSpec((1, nt, D_ATTN, T), lambda b, i: (b, 0, 0, 0)),
            _full((N_META, N_HEADS * HEAD_PAD)), _full((D_ATTN, META_ROWS)), _full((1, D_ATTN)),
        ],
        out_specs=pl.BlockSpec((1, Q_TILES * T, D_ATTN), lambda b, i: (b, i, 0)),
        out_shape=jax.ShapeDtypeStruct((B, S, D_ATTN), bf16),
        scratch_shapes=[pltpu.VMEM((n_lanes, 1, T), f32),
                        pltpu.VMEM((n_lanes * V_AUG, T), f32),
                        pltpu.VMEM((2, n_lanes, 1, T), f32),
                        pltpu.VMEM((2, n_lanes, 1, T), f32),
                        pltpu.VMEM((2, n_lanes, T, T), f32),
                        pltpu.VMEM((2, n_lanes, T, T), bf16)],
        compiler_params=params,
        name="block_causal_attn",
    )(qt4, k4, vt4, k_meta, v_meta, attn_g)

    nf = S // T_FFN
    out = pl.pallas_call(
        _ffn_kernel,
        grid=(B, nf),
        in_specs=[
            pl.BlockSpec((1, T_FFN, D), lambda b, t: (b, t, 0)),
            _full((N_FF_SLABS, SUBLANES, LANES)),
            pl.BlockSpec((1, T_FFN, D_CONV), lambda b, t: (b, t, 0)),
            pl.BlockSpec((1, T_FFN, D_ATTN), lambda b, t: (b, t, 0)),
            _full((D_CONV + D_ATTN, D)), _full((1, D)), _full((D, 2 * D_FF)),
            _full((FFN_CONV_WIDTH, 2 * D_FF)), _full((1, 2 * D_FF)), _full((D_FF, D)),
            _full((1, D)),
        ],
        out_specs=pl.BlockSpec((1, T_FFN, D), lambda b, t: (b, t, 0)),
        out_shape=jax.ShapeDtypeStruct((B, S, D), x.dtype),
        scratch_shapes=[pltpu.VMEM((N_FF_SLABS, SUBLANES + T_FFN, LANES), f32),
                        pltpu.VMEM((T_FFN, D), f32)],
        compiler_params=params,
        name="outproj_convffn",
    )(x, up_halo, u_n, o_n, w_out_b, ffn_g, w_up_b, ffn_conv_w[0].astype(f32),
      row2(ffn_conv_b[0]), w_down_b, row2(final_norm_g))
    return out
```

```python
import jax
import jax.numpy as jnp
import numpy as np
from jax import lax
from jax.experimental import pallas as pl
from jax.experimental.pallas import tpu as pltpu

D_MODEL = 1024
CHUNK = 64
N_META = 16
D_CONV = 512
CONV_WIDTH = 31
N_HEADS = 8
QK_NOPE = 64
QK_ROPE = 32
V_HEAD = 64
D_ATTN = N_HEADS * V_HEAD
Q_LORA = 384
KV_LORA = 256
ROPE_THETA = 10000.0
D_FF = 2816
FFN_CONV_WIDTH = 3
EPS = 1e-6
NEG = -1e30

LANES = 128
SUBLANES = 8
T = 256
T_PROJ = 512
T_FFN = 512
Q_TILES = 2
META_ROWS = 128
HEAD_PAD = 128
V_AUG = V_HEAD + 16
HALO = 32
CONV_ROWS = 64
ROPE_HALF = QK_ROPE // 2
D_IN_PAD = 2 * D_CONV + Q_LORA + KV_LORA + LANES
KR_OFF = 2 * D_CONV + Q_LORA + KV_LORA
N_CONV_SLABS = D_CONV // LANES
N_FF_SLABS = 2 * D_FF // LANES
FF_CHUNK = 256
N_FF_CHUNKS = D_FF // FF_CHUNK
FF_AHEAD = 3
VMEM_LIMIT = 56 * 1024 * 1024
Q_SCALE = (QK_NOPE + QK_ROPE) ** -0.5 * 1.4426950408889634
NT_DIMS = (((1,), (1,)), ((), ()))


def _rms(x, g):
    return x * lax.rsqrt(jnp.mean(x * x, axis=-1, keepdims=True) + EPS) * g


def _head_slices():
    return [(slice(hd * HEAD_PAD, (hd + 1) * HEAD_PAD), slice(hd * V_HEAD, (hd + 1) * V_HEAD),
             slice(hd * V_AUG, (hd + 1) * V_AUG)) for hd in range(N_HEADS)]


def _with_ones(vt):
    return jnp.concatenate([vt, jnp.ones((V_AUG - V_HEAD, vt.shape[1]), jnp.bfloat16)], axis=0)


def _conv_group(n, w_in_ref, xbuf, cw_ref, cb_ref, ln_g_ref, ln_b_ref, cog_ref):
    rows = n.shape[0]
    block = min(CONV_ROWS, rows)
    conv_slabs = []
    for s in range(N_CONV_SLABS):
        lanes = slice(s * LANES, (s + 1) * LANES)
        w_slab = jnp.concatenate([w_in_ref[:, lanes],
                                  w_in_ref[:, D_CONV + s * LANES:D_CONV + (s + 1) * LANES]], axis=1)
        z = jnp.dot(n, w_slab, preferred_element_type=jnp.float32)
        xbuf[s, HALO:HALO + rows, :] = z[:, :LANES] * jax.nn.sigmoid(z[:, LANES:])
        blocks = []
        for r0 in range(0, rows, block):
            acc = jnp.zeros((block, LANES), jnp.float32) + cb_ref[:, lanes]
            for k in range(CONV_WIDTH):
                off = HALO - (CONV_WIDTH - 1) + k + r0
                acc = acc + cw_ref[k:k + 1, lanes] * xbuf[s, off:off + block, :]
            blocks.append(acc)
        conv_slabs.append(jnp.concatenate(blocks, axis=0))
        xbuf[s, 0:HALO, :] = xbuf[s, rows:rows + HALO, :]
    c = jnp.concatenate(conv_slabs, axis=-1)
    mu = jnp.mean(c, axis=-1, keepdims=True)
    var = jnp.mean(jnp.square(c - mu), axis=-1, keepdims=True)
    c = (c - mu) * lax.rsqrt(var + EPS) * ln_g_ref[...] + ln_b_ref[...]
    c = c * jax.nn.sigmoid(c)
    return _rms(c, cog_ref[...]).astype(jnp.bfloat16)


def _qkv(n, w_in_ref, q_g_ref, w_uqt_ref, kv_g_ref, w_uk_ref, w_uvt_ref, kc, ks1, ks2, qcos, qsin):
    c_q = jnp.dot(n, w_in_ref[:, 2 * D_CONV:2 * D_CONV + Q_LORA],
                  preferred_element_type=jnp.float32)
    z_kv = jnp.dot(n, w_in_ref[:, 2 * D_CONV + Q_LORA:D_IN_PAD], preferred_element_type=jnp.float32)
    c_kv = z_kv[:, :KV_LORA]
    k_r = pltpu.roll(z_kv[:, KV_LORA:], QK_NOPE, 1)
    qn = _rms(c_q, q_g_ref[...]).astype(jnp.bfloat16)
    kvn = _rms(c_kv, kv_g_ref[...]).astype(jnp.bfloat16)

    k_rot = (k_r * kc + pltpu.roll(k_r, ROPE_HALF, 1) * ks1
             + pltpu.roll(k_r, LANES - ROPE_HALF, 1) * ks2)
    k_nope = jnp.dot(kvn, w_uk_ref[...], preferred_element_type=jnp.float32)
    k = jnp.concatenate(
        [(k_nope[:, hd * HEAD_PAD:(hd + 1) * HEAD_PAD] + k_rot).astype(jnp.bfloat16)
         for hd in range(N_HEADS)], axis=-1)

    vt = lax.dot_general(w_uvt_ref[...], kvn, NT_DIMS,
                         preferred_element_type=jnp.float32).astype(jnp.bfloat16)
    qt = lax.dot_general(w_uqt_ref[...], qn, NT_DIMS, preferred_element_type=jnp.float32)
    pieces = []
    for hd in range(N_HEADS):
        b0 = hd * HEAD_PAD
        x1 = qt[b0 + QK_NOPE:b0 + QK_NOPE + ROPE_HALF]
        x2 = qt[b0 + QK_NOPE + ROPE_HALF:b0 + QK_NOPE + QK_ROPE]
        pieces += [qt[b0:b0 + QK_NOPE] * Q_SCALE, x1 * qcos - x2 * qsin, x2 * qcos + x1 * qsin,
                   jnp.zeros((HEAD_PAD - QK_NOPE - QK_ROPE, qt.shape[1]), jnp.float32)]
    return k, jnp.concatenate(pieces, axis=0).astype(jnp.bfloat16), vt


def _meta_keys_softmax(km, vm, qt):
    heads = _head_slices()
    nq = qt.shape[1]
    scores = [jnp.dot(km[:, qsl], qt[qsl, :], preferred_element_type=jnp.float32)
              for qsl, _, _ in heads]
    maxes, probs = [], []
    for s in scores:
        m = jnp.max(s, axis=0, keepdims=True)
        maxes.append(m)
        probs.append(jnp.concatenate(
            [jnp.exp2(s - m).astype(jnp.bfloat16),
             jnp.zeros((LANES - N_META, nq), jnp.bfloat16)], axis=0))
    accs = [jnp.dot(_with_ones(vm[vsl, :]), p, preferred_element_type=jnp.float32)
            for (_, vsl, _), p in zip(heads, probs)]
    return maxes, accs


def _attn_output(acc_of_head, g):
    ot = jnp.concatenate([a[:V_HEAD] / a[V_HEAD:V_HEAD + 1] for a in acc_of_head], axis=0)
    ot = ot * lax.rsqrt(jnp.mean(ot * ot, axis=0, keepdims=True) + EPS)
    return (ot.T * g).astype(jnp.bfloat16)


def _meta_kernel(meta_ref, mix_g_ref, w_in_ref, q_g_ref, w_uqt_ref, kv_g_ref, w_uk_ref, w_uvt_ref,
                 cw_ref, cb_ref, ln_g_ref, ln_b_ref, cog_ref, kc_ref, ks1_ref, ks2_ref,
                 qcos_ref, qsin_ref, attn_g_ref, w_out_ref, ffn_g_ref, w_up_ref,
                 halo_ref, km_ref, vm_ref, uph_ref, xbuf):
    h = jnp.concatenate(
        [meta_ref[...], jnp.zeros((META_ROWS - N_META, D_MODEL), jnp.float32)], axis=0)
    n = _rms(h, mix_g_ref[...]).astype(jnp.bfloat16)

    xbuf[:, 0:HALO, :] = jnp.zeros((N_CONV_SLABS, HALO, LANES), jnp.float32)
    u_n = _conv_group(n, w_in_ref, xbuf, cw_ref, cb_ref, ln_g_ref, ln_b_ref, cog_ref)
    halo_ref[:, 0:HALO - N_META, :] = jnp.zeros((N_CONV_SLABS, HALO - N_META, LANES), jnp.float32)
    halo_ref[:, HALO - N_META:HALO, :] = xbuf[:, HALO:HALO + N_META, :]

    k, qt, vt = _qkv(n, w_in_ref, q_g_ref, w_uqt_ref, kv_g_ref, w_uk_ref, w_uvt_ref,
                     kc_ref[...], ks1_ref[...], ks2_ref[...], qcos_ref[...], qsin_ref[...])
    km = k[0:N_META]
    key_is_meta = lax.broadcasted_iota(jnp.int32, (D_ATTN, META_ROWS), 1) < N_META
    vm = jnp.where(key_is_meta, vt, jnp.zeros_like(vt))
    km_ref[...] = km
    vm_ref[...] = vm

    _, accs = _meta_keys_softmax(km, vm, qt)
    o_n = _attn_output(accs, attn_g_ref[...])

    h1 = (h + jnp.dot(u_n, w_out_ref[0:D_CONV, :], preferred_element_type=jnp.float32)
          + jnp.dot(o_n, w_out_ref[D_CONV:, :], preferred_element_type=jnp.float32))
    n2 = _rms(h1, ffn_g_ref[...]).astype(jnp.bfloat16)
    for c in range(2 * D_FF // FF_CHUNK):
        up = jnp.dot(n2, w_up_ref[:, c * FF_CHUNK:(c + 1) * FF_CHUNK],
                     preferred_element_type=jnp.float32)
        for s in range(FF_CHUNK // LANES):
            uph_ref[c * (FF_CHUNK // LANES) + s] = up[N_META - SUBLANES:N_META,
                                                      s * LANES:(s + 1) * LANES]


def _proj_kernel(x_ref, halo_ref, mix_g_ref, w_in_ref, q_g_ref, w_uqt_ref, kv_g_ref,
                 w_uk_ref, w_uvt_ref, cw_ref, cb_ref, ln_g_ref, ln_b_ref, cog_ref,
                 kc_ref, ks1_ref, ks2_ref, qcos_ref, qsin_ref,
                 u_ref, k_ref, qt_ref, vt_ref, xbuf):
    @pl.when(pl.program_id(1) == 0)
    def _():
        xbuf[:, 0:HALO, :] = halo_ref[...]

    n = _rms(x_ref[0], mix_g_ref[...]).astype(jnp.bfloat16)
    u_ref[0] = _conv_group(n, w_in_ref, xbuf, cw_ref, cb_ref, ln_g_ref, ln_b_ref, cog_ref)
    k, qt, vt = _qkv(n, w_in_ref, q_g_ref, w_uqt_ref, kv_g_ref, w_uk_ref, w_uvt_ref,
                     kc_ref[...], ks1_ref[...], ks2_ref[...], qcos_ref[...], qsin_ref[...])
    for j in range(T_PROJ // T):
        k_ref[0, j] = k[j * T:(j + 1) * T]
        qt_ref[0, j] = qt[:, j * T:(j + 1) * T]
        vt_ref[0, j] = vt[:, j * T:(j + 1) * T]


def _attn_kernel(qt_ref, qtn_ref, k_ref, vt_ref, km_ref, vm_ref, g_ref, o_ref,
                 m_sc, acc_sc, alpha_sc, smax_sc, s_sc, p_sc):
    i = pl.program_id(1)
    heads = _head_slices()
    lanes = [(qh, qh * N_HEADS + hd, qsl, vsl, slice((qh * N_HEADS + hd) * V_AUG,
                                                   (qh * N_HEADS + hd + 1) * V_AUG))
             for qh in range(Q_TILES) for hd, (qsl, vsl, _) in enumerate(heads)]
    FULL, DIAG, SKIP = "full", "diag", "skip"
    all_full = [FULL] * len(lanes)
    tile_a = [DIAG if qh == 0 else FULL for qh, *_ in lanes]
    tile_b = [SKIP if qh == 0 else DIAG for qh, *_ in lanes]

    for qh in range(Q_TILES):
        maxes, accs = _meta_keys_softmax(km_ref[...], vm_ref[...], qt_ref[0, qh])
        for hd in range(N_HEADS):
            _, ln, _, _, asl = lanes[qh * N_HEADS + hd]
            m_sc[ln] = maxes[hd]
            acc_sc[asl, :] = accs[hd]

    def scores(n, slot, modes, q_ref=qt_ref):
        for (qh, ln, qsl, _, _), mode in zip(lanes, modes):
            if mode == SKIP:
                continue
            s = jnp.dot(k_ref[0, n - 1, :, qsl], q_ref[0, qh, qsl, :],
                        preferred_element_type=jnp.float32)
            s_sc[slot, ln] = s
            smax_sc[slot, ln] = jnp.max(s, axis=0, keepdims=True)

    def softmax(slot, modes):
        if DIAG in modes:
            key_chunk = lax.broadcasted_iota(jnp.int32, (T, T), 0) // CHUNK
            qry_chunk = lax.broadcasted_iota(jnp.int32, (T, T), 1) // CHUNK
            visible = key_chunk <= qry_chunk
        for (_, ln, _, _, _), mode in zip(lanes, modes):
            if mode == SKIP:
                continue
            s = s_sc[slot, ln]
            if mode == DIAG:
                s = jnp.where(visible, s, NEG)
                tile_max = jnp.max(s, axis=0, keepdims=True)
            else:
                tile_max = smax_sc[slot, ln]
            m = m_sc[ln]
            m_new = jnp.maximum(m, tile_max)
            alpha_sc[slot, ln] = jnp.exp2(m - m_new)
            p_sc[slot, ln] = jnp.exp2(s - m_new).astype(jnp.bfloat16)
            m_sc[ln] = m_new

    def values(n, slot, modes):
        for (_, ln, _, vsl, asl), mode in zip(lanes, modes):
            if mode == SKIP:
                continue
            acc_sc[asl, :] = alpha_sc[slot, ln] * acc_sc[asl, :] + jnp.dot(
                _with_ones(vt_ref[0, jnp.maximum(n - 1, 0), vsl, :]), p_sc[slot, ln],
                preferred_element_type=jnp.float32)

    def stage(n, slot, modes, prev_modes, next_modes):
        for qsel in range(Q_TILES):
            sel = lambda ms: [m if qh == qsel else SKIP for (qh, *_), m in zip(lanes, ms)]
            softmax(slot, sel(modes))
            if next_modes is not None:
                scores(n + 1, 1 - slot, sel(next_modes))
            values(n - 1, 1 - slot, sel(prev_modes))

    p_sc[0] = jnp.zeros((len(lanes), T, T), jnp.bfloat16)
    alpha_sc[0] = jnp.ones((len(lanes), 1, T), jnp.float32)

    @pl.when(i == 0)
    def _():
        scores(1, 1, all_full)

    def pair(kk, carry):
        n = 2 * kk + 1
        stage(n, 1, all_full, all_full, all_full)
        stage(n + 1, 0, all_full, all_full, all_full)
        return carry

    lax.fori_loop(0, i, pair, 0)
    n_a = 2 * i + 1
    stage(n_a, 1, tile_a, all_full, tile_b)
    stage(n_a + 1, 0, tile_b, tile_a, None)
    values(n_a + 1, 0, tile_b)
    scores(1, 1, all_full, q_ref=qtn_ref)

    for qh in range(Q_TILES):
        o_ref[0, qh * T:(qh + 1) * T, :] = _attn_output(
            [acc_sc[asl, :] for q, _, _, _, asl in lanes if q == qh], g_ref[...])


def _ffn_kernel(x_ref, uph_ref, u_ref, o_ref, w_out_ref, ffn_g_ref, w_up_ref, fcw_ref,
                fcb_ref, w_down_ref, fin_g_ref, out_ref, upbuf, acc_ref):
    t = pl.program_id(1)
    h1 = (x_ref[0]
          + jnp.dot(u_ref[0], w_out_ref[0:D_CONV, :], preferred_element_type=jnp.float32)
          + jnp.dot(o_ref[0], w_out_ref[D_CONV:, :], preferred_element_type=jnp.float32))
    acc_ref[...] = h1
    n2 = _rms(h1, ffn_g_ref[...]).astype(jnp.bfloat16)

    @pl.when(t == 0)
    def _():
        upbuf[:, 0:SUBLANES, :] = uph_ref[...]

    def up_proj(c):
        return tuple(
            jnp.dot(n2, w_up_ref[:, col0:col0 + FF_CHUNK], preferred_element_type=jnp.float32)
            for col0 in (c * FF_CHUNK, D_FF + c * FF_CHUNK))

    def conv3(up, col0):
        outs = []
        for s in range(FF_CHUNK // LANES):
            slab = col0 // LANES + s
            cols = slice(col0 + s * LANES, col0 + (s + 1) * LANES)
            upbuf[slab, SUBLANES:SUBLANES + T_FFN, :] = up[:, s * LANES:(s + 1) * LANES]
            y = fcb_ref[:, cols] + jnp.zeros((T_FFN, LANES), jnp.float32)
            for k in range(FFN_CONV_WIDTH):
                off = SUBLANES - (FFN_CONV_WIDTH - 1) + k
                y = y + fcw_ref[k:k + 1, cols] * upbuf[slab, off:off + T_FFN, :]
            outs.append(y)
            upbuf[slab, 0:SUBLANES, :] = upbuf[slab, T_FFN:T_FFN + SUBLANES, :]
        return jnp.concatenate(outs, axis=-1)

    ups = [up_proj(c) for c in range(FF_AHEAD)]
    for c in range(N_FF_CHUNKS):
        if c + FF_AHEAD < N_FF_CHUNKS:
            ups.append(up_proj(c + FF_AHEAD))
        up_g, up_val = ups.pop(0)
        g = conv3(up_g, c * FF_CHUNK)
        val = conv3(up_val, D_FF + c * FF_CHUNK)
        act = (g * jax.nn.sigmoid(g) * val).astype(jnp.bfloat16)
        contrib = jnp.dot(act, w_down_ref[c * FF_CHUNK:(c + 1) * FF_CHUNK, :],
                          preferred_element_type=jnp.float32)
        acc_ref[...] += contrib
    out_ref[0] = _rms(acc_ref[...], fin_g_ref[...]).astype(out_ref.dtype)


def _full(shape):
    return pl.BlockSpec(shape, lambda *_: (0,) * len(shape))


def _rope_tables(pos):
    inv_freq = 1.0 / (ROPE_THETA ** (np.arange(0, QK_ROPE, 2, dtype=np.float64) / QK_ROPE))
    ang = np.asarray(pos, np.float64)[:, None] * inv_freq[None, :]
    cos, sin = np.cos(ang), np.sin(ang)
    zl = lambda n: np.zeros((len(pos), n))
    kc = np.concatenate([zl(QK_NOPE), cos, cos, zl(LANES - QK_NOPE - QK_ROPE)], axis=1)
    ks1 = np.concatenate([zl(QK_NOPE + ROPE_HALF), sin, zl(LANES - QK_NOPE - QK_ROPE)], axis=1)
    ks2 = np.concatenate([zl(QK_NOPE), -sin, zl(LANES - QK_NOPE - ROPE_HALF)], axis=1)
    tabs = (kc, ks1, ks2, (cos * Q_SCALE).T, (sin * Q_SCALE).T)
    return tuple(jnp.asarray(t.astype(np.float32)) for t in tabs)


def kernel(x, meta_tokens, mix_norm_g, w_in, q_norm_g, w_uq, kv_norm_g, w_ukv, conv_w, conv_b,
           conv_ln_g, conv_ln_b, conv_out_g, attn_out_g, w_out, ffn_norm_g, w_ffn_up,
           ffn_conv_w, ffn_conv_b, w_ffn_down, final_norm_g):
    B, S, D = x.shape
    assert D == D_MODEL and S % (Q_TILES * T) == 0 and S % T_FFN == 0 and mix_norm_g.shape[0] == 1
    assert meta_tokens.shape == (N_META, D)
    nt = S // T
    bf16, f32 = jnp.bfloat16, jnp.float32
    row2 = lambda v: v.reshape(1, -1).astype(f32)

    w_in_p = jnp.pad(w_in[0].astype(bf16), ((0, 0), (0, LANES - QK_ROPE)))
    w_uq3 = w_uq[0].reshape(Q_LORA, N_HEADS, QK_NOPE + QK_ROPE)
    w_uqt = jnp.pad(w_uq3, ((0, 0), (0, 0), (0, HEAD_PAD - QK_NOPE - QK_ROPE))).reshape(
        Q_LORA, N_HEADS * HEAD_PAD).T.astype(bf16)
    w_ukv3 = w_ukv[0].reshape(KV_LORA, N_HEADS, QK_NOPE + V_HEAD)
    w_uk = jnp.pad(w_ukv3[:, :, :QK_NOPE], ((0, 0), (0, 0), (0, HEAD_PAD - QK_NOPE))).reshape(
        KV_LORA, N_HEADS * HEAD_PAD).astype(bf16)
    w_uvt = w_ukv3[:, :, QK_NOPE:].reshape(KV_LORA, D_ATTN).T.astype(bf16)
    w_out_b = w_out[0].astype(bf16)
    w_up_b = w_ffn_up[0].astype(bf16)
    w_down_b = w_ffn_down[0].astype(bf16)
    mix_g, q_g, kv_g = row2(mix_norm_g[0]), row2(q_norm_g[0]), row2(kv_norm_g[0])
    cw, cb = conv_w[0].astype(f32), row2(conv_b[0])
    ln_g, ln_b, cog = row2(conv_ln_g[0]), row2(conv_ln_b[0]), row2(conv_out_g[0])
    attn_g, ffn_g = row2(attn_out_g[0]), row2(ffn_norm_g[0])

    meta_tabs = _rope_tables(np.arange(META_ROWS))
    tabs = _rope_tables(np.arange(S) + N_META)

    proj_w = [mix_g, w_in_p, q_g, w_uqt, kv_g, w_uk, w_uvt]
    proj_specs = [_full((1, D)), _full((D, D_IN_PAD)), _full((1, Q_LORA)),
                  _full((N_HEADS * HEAD_PAD, Q_LORA)), _full((1, KV_LORA)),
                  _full((KV_LORA, N_HEADS * HEAD_PAD)), _full((D_ATTN, KV_LORA))]
    conv_w_ = [cw, cb, ln_g, ln_b, cog]
    conv_specs = [_full((CONV_WIDTH, D_CONV)), _full((1, D_CONV)), _full((1, D_CONV)),
                  _full((1, D_CONV)), _full((1, D_CONV))]
    front_w, front_specs = proj_w + conv_w_, proj_specs + conv_specs
    seq = ("arbitrary", "arbitrary")

    conv_halo, k_meta, v_meta, up_halo = pl.pallas_call(
        _meta_kernel,
        grid=(1,),
        in_specs=[_full((N_META, D))] + front_specs + [
            _full((META_ROWS, LANES)), _full((META_ROWS, LANES)), _full((META_ROWS, LANES)),
            _full((ROPE_HALF, META_ROWS)), _full((ROPE_HALF, META_ROWS)),
            _full((1, D_ATTN)), _full((D_CONV + D_ATTN, D)), _full((1, D)), _full((D, 2 * D_FF))],
        out_specs=[_full((N_CONV_SLABS, HALO, LANES)), _full((N_META, N_HEADS * HEAD_PAD)),
                   _full((D_ATTN, META_ROWS)), _full((N_FF_SLABS, SUBLANES, LANES))],
        out_shape=[jax.ShapeDtypeStruct((N_CONV_SLABS, HALO, LANES), f32),
                   jax.ShapeDtypeStruct((N_META, N_HEADS * HEAD_PAD), bf16),
                   jax.ShapeDtypeStruct((D_ATTN, META_ROWS), bf16),
                   jax.ShapeDtypeStruct((N_FF_SLABS, SUBLANES, LANES), f32)],
        scratch_shapes=[pltpu.VMEM((N_CONV_SLABS, HALO + META_ROWS, LANES), f32)],
        compiler_params=pltpu.CompilerParams(dimension_semantics=("arbitrary",),
                                             vmem_limit_bytes=VMEM_LIMIT),
        name="meta_tokens",
    )(meta_tokens.astype(f32), *front_w, *meta_tabs, attn_g, w_out_b, ffn_g, w_up_b)

    params = pltpu.CompilerParams(dimension_semantics=seq, vmem_limit_bytes=VMEM_LIMIT)
    u_n, k4, qt4, vt4 = pl.pallas_call(
        _proj_kernel,
        grid=(B, S // T_PROJ),
        in_specs=[pl.BlockSpec((1, T_PROJ, D), lambda b, t: (b, t, 0)),
                  _full((N_CONV_SLABS, HALO, LANES))] + front_specs + [
            pl.BlockSpec((T_PROJ, LANES), lambda b, t: (t, 0)),
            pl.BlockSpec((T_PROJ, LANES), lambda b, t: (t, 0)),
            pl.BlockSpec((T_PROJ, LANES), lambda b, t: (t, 0)),
            pl.BlockSpec((ROPE_HALF, T_PROJ), lambda b, t: (0, t)),
            pl.BlockSpec((ROPE_HALF, T_PROJ), lambda b, t: (0, t))],
        out_specs=[
            pl.BlockSpec((1, T_PROJ, D_CONV), lambda b, t: (b, t, 0)),
            pl.BlockSpec((1, T_PROJ // T, T, N_HEADS * HEAD_PAD), lambda b, t: (b, t, 0, 0)),
            pl.BlockSpec((1, T_PROJ // T, N_HEADS * HEAD_PAD, T), lambda b, t: (b, t, 0, 0)),
            pl.BlockSpec((1, T_PROJ // T, D_ATTN, T), lambda b, t: (b, t, 0, 0)),
        ],
        out_shape=[
            jax.ShapeDtypeStruct((B, S, D_CONV), bf16),
            jax.ShapeDtypeStruct((B, nt, T, N_HEADS * HEAD_PAD), bf16),
            jax.ShapeDtypeStruct((B, nt, N_HEADS * HEAD_PAD, T), bf16),
            jax.ShapeDtypeStruct((B, nt, D_ATTN, T), bf16),
        ],
        scratch_shapes=[pltpu.VMEM((N_CONV_SLABS, HALO + T_PROJ, LANES), f32)],
        compiler_params=params,
        name="proj_conv_qkv",
    )(x, conv_halo, *front_w, *tabs)

    n_lanes = Q_TILES * N_HEADS
    o_n = pl.pallas_call(
        _attn_kernel,
        grid=(B, nt // Q_TILES),
        in_specs=[
            pl.BlockSpec((1, Q_TILES, N_HEADS * HEAD_PAD, T), lambda b, i: (b, i, 0, 0)),
            pl.BlockSpec((1, Q_TILES, N_HEADS * HEAD_PAD, T),
                         lambda b, i: (b, jnp.minimum(i + 1, nt // Q_TILES - 1), 0, 0)),
            pl.BlockSpec((1, nt, T, N_HEADS * HEAD_PAD), lambda b, i: (b, 0, 0, 0)),
            pl.BlockSpec((1, nt, D_ATTN, T), lambda b, i: (b, 0, 0, 0)),
            _full((N_META, N_HEADS * HEAD_PAD)), _full((D_ATTN, META_ROWS)), _full((1, D_ATTN)),
        ],
        out_specs=pl.BlockSpec((1, Q_TILES * T, D_ATTN), lambda b, i: (b, i, 0)),
        out_shape=jax.ShapeDtypeStruct((B, S, D_ATTN), bf16),
        scratch_shapes=[pltpu.VMEM((n_lanes, 1, T), f32),
                        pltpu.VMEM((n_lanes * V_AUG, T), f32),
                        pltpu.VMEM((2, n_lanes, 1, T), f32),
                        pltpu.VMEM((2, n_lanes, 1, T), f32),
                        pltpu.VMEM((2, n_lanes, T, T), f32),
                        pltpu.VMEM((2, n_lanes, T, T), bf16)],
        compiler_params=params,
        name="block_causal_attn",
    )(qt4, qt4, k4, vt4, k_meta, v_meta, attn_g)

    nf = S // T_FFN
    out = pl.pallas_call(
        _ffn_kernel,
        grid=(B, nf),
        in_specs=[
            pl.BlockSpec((1, T_FFN, D), lambda b, t: (b, t, 0)),
            _full((N_FF_SLABS, SUBLANES, LANES)),
            pl.BlockSpec((1, T_FFN, D_CONV), lambda b, t: (b, t, 0)),
            pl.BlockSpec((1, T_FFN, D_ATTN), lambda b, t: (b, t, 0)),
            _full((D_CONV + D_ATTN, D)), _full((1, D)), _full((D, 2 * D_FF)),
            _full((FFN_CONV_WIDTH, 2 * D_FF)), _full((1, 2 * D_FF)), _full((D_FF, D)),
            _full((1, D)),
        ],
        out_specs=pl.BlockSpec((1, T_FFN, D), lambda b, t: (b, t, 0)),
        out_shape=jax.ShapeDtypeStruct((B, S, D), x.dtype),
        scratch_shapes=[pltpu.VMEM((N_FF_SLABS, SUBLANES + T_FFN, LANES), f32),
                        pltpu.VMEM((T_FFN, D), f32)],
        compiler_params=params,
        name="outproj_convffn",
    )(x, up_halo, u_n, o_n, w_out_b, ffn_g, w_up_b, ffn_conv_w[0].astype(f32),
      row2(ffn_conv_b[0]), w_down_b, row2(final_norm_g))
    return out
```

```python
import jax
import jax.numpy as jnp
import numpy as np
from jax import lax
from jax.experimental import pallas as pl
from jax.experimental.pallas import tpu as pltpu

D_MODEL = 1024
CHUNK = 64
N_META = 16
D_CONV = 512
CONV_WIDTH = 31
N_HEADS = 8
QK_NOPE = 64
QK_ROPE = 32
V_HEAD = 64
D_ATTN = N_HEADS * V_HEAD
Q_LORA = 384
KV_LORA = 256
ROPE_THETA = 10000.0
D_FF = 2816
FFN_CONV_WIDTH = 3
EPS = 1e-6
NEG = -1e30

LANES = 128
SUBLANES = 8
T = 256
T_PROJ = 512
T_FFN = 512
Q_TILES = 2
META_ROWS = 128
HEAD_PAD = 128
V_AUG = V_HEAD + 16
HALO = 32
CONV_ROWS = 64
ROPE_HALF = QK_ROPE // 2
D_IN_PAD = 2 * D_CONV + Q_LORA + KV_LORA + LANES
KR_OFF = 2 * D_CONV + Q_LORA + KV_LORA
N_CONV_SLABS = D_CONV // LANES
N_FF_SLABS = 2 * D_FF // LANES
FF_CHUNK = 256
N_FF_CHUNKS = D_FF // FF_CHUNK
FF_AHEAD = 3
VMEM_LIMIT = 56 * 1024 * 1024
Q_SCALE = (QK_NOPE + QK_ROPE) ** -0.5 * 1.4426950408889634
NT_DIMS = (((1,), (1,)), ((), ()))


def _rms(x, g):
    return x * lax.rsqrt(jnp.mean(x * x, axis=-1, keepdims=True) + EPS) * g


def _head_slices():
    return [(slice(hd * HEAD_PAD, (hd + 1) * HEAD_PAD), slice(hd * V_HEAD, (hd + 1) * V_HEAD),
             slice(hd * V_AUG, (hd + 1) * V_AUG)) for hd in range(N_HEADS)]


def _with_ones(vt):
    return jnp.concatenate([vt, jnp.ones((V_AUG - V_HEAD, vt.shape[1]), jnp.bfloat16)], axis=0)


def _conv_group(n, w_in_ref, xbuf, cw_ref, cb_ref, ln_g_ref, ln_b_ref, cog_ref):
    rows = n.shape[0]
    block = min(CONV_ROWS, rows)
    conv_slabs = []
    for s in range(N_CONV_SLABS):
        lanes = slice(s * LANES, (s + 1) * LANES)
        w_slab = jnp.concatenate([w_in_ref[:, lanes],
                                  w_in_ref[:, D_CONV + s * LANES:D_CONV + (s + 1) * LANES]], axis=1)
        z = jnp.dot(n, w_slab, preferred_element_type=jnp.float32)
        xbuf[s, HALO:HALO + rows, :] = z[:, :LANES] * jax.nn.sigmoid(z[:, LANES:])
        blocks = []
        for r0 in range(0, rows, block):
            acc = jnp.zeros((block, LANES), jnp.float32) + cb_ref[:, lanes]
            for k in range(CONV_WIDTH):
                off = HALO - (CONV_WIDTH - 1) + k + r0
                acc = acc + cw_ref[k:k + 1, lanes] * xbuf[s, off:off + block, :]
            blocks.append(acc)
        conv_slabs.append(jnp.concatenate(blocks, axis=0))
        xbuf[s, 0:HALO, :] = xbuf[s, rows:rows + HALO, :]
    c = jnp.concatenate(conv_slabs, axis=-1)
    mu = jnp.mean(c, axis=-1, keepdims=True)
    var = jnp.mean(jnp.square(c - mu), axis=-1, keepdims=True)
    c = (c - mu) * lax.rsqrt(var + EPS) * ln_g_ref[...] + ln_b_ref[...]
    c = c * jax.nn.sigmoid(c)
    return _rms(c, cog_ref[...]).astype(jnp.bfloat16)


def _qkv(n, w_in_ref, q_g_ref, w_uqt_ref, kv_g_ref, w_uk_ref, w_uvt_ref, kc, ks1, ks2, qcos, qsin):
    c_q = jnp.dot(n, w_in_ref[:, 2 * D_CONV:2 * D_CONV + Q_LORA],
                  preferred_element_type=jnp.float32)
    z_kv = jnp.dot(n, w_in_ref[:, 2 * D_CONV + Q_LORA:D_IN_PAD], preferred_element_type=jnp.float32)
    c_kv = z_kv[:, :KV_LORA]
    k_r = pltpu.roll(z_kv[:, KV_LORA:], QK_NOPE, 1)
    qn = _rms(c_q, q_g_ref[...]).astype(jnp.bfloat16)
    kvn = _rms(c_kv, kv_g_ref[...]).astype(jnp.bfloat16)

    k_rot = (k_r * kc + pltpu.roll(k_r, ROPE_HALF, 1) * ks1
             + pltpu.roll(k_r, LANES - ROPE_HALF, 1) * ks2)
    k_nope = jnp.dot(kvn, w_uk_ref[...], preferred_element_type=jnp.float32)
    k = jnp.concatenate(
        [(k_nope[:, hd * HEAD_PAD:(hd + 1) * HEAD_PAD] + k_rot).astype(jnp.bfloat16)
         for hd in range(N_HEADS)], axis=-1)

    vt = lax.dot_general(w_uvt_ref[...], kvn, NT_DIMS,
                         preferred_element_type=jnp.float32).astype(jnp.bfloat16)
    qt = lax.dot_general(w_uqt_ref[...], qn, NT_DIMS, preferred_element_type=jnp.float32)
    pieces = []
    for hd in range(N_HEADS):
        b0 = hd * HEAD_PAD
        x1 = qt[b0 + QK_NOPE:b0 + QK_NOPE + ROPE_HALF]
        x2 = qt[b0 + QK_NOPE + ROPE_HALF:b0 + QK_NOPE + QK_ROPE]
        pieces += [qt[b0:b0 + QK_NOPE] * Q_SCALE, x1 * qcos - x2 * qsin, x2 * qcos + x1 * qsin,
                   jnp.zeros((HEAD_PAD - QK_NOPE - QK_ROPE, qt.shape[1]), jnp.float32)]
    return k, jnp.concatenate(pieces, axis=0).astype(jnp.bfloat16), vt


def _meta_keys_softmax(km, vm, qt):
    heads = _head_slices()
    nq = qt.shape[1]
    scores = [jnp.dot(km[:, qsl], qt[qsl, :], preferred_element_type=jnp.float32)
              for qsl, _, _ in heads]
    maxes, probs = [], []
    for s in scores:
        m = jnp.max(s, axis=0, keepdims=True)
        maxes.append(m)
        probs.append(jnp.concatenate(
            [jnp.exp2(s - m).astype(jnp.bfloat16),
             jnp.zeros((LANES - N_META, nq), jnp.bfloat16)], axis=0))
    accs = [jnp.dot(_with_ones(vm[vsl, :]), p, preferred_element_type=jnp.float32)
            for (_, vsl, _), p in zip(heads, probs)]
    return maxes, accs


def _attn_output(acc_of_head, g):
    ot = jnp.concatenate([a[:V_HEAD] / a[V_HEAD:V_HEAD + 1] for a in acc_of_head], axis=0)
    ot = ot * lax.rsqrt(jnp.mean(ot * ot, axis=0, keepdims=True) + EPS)
    return (ot.T * g).astype(jnp.bfloat16)


def _meta_kernel(meta_ref, mix_g_ref, w_in_ref, q_g_ref, w_uqt_ref, kv_g_ref, w_uk_ref, w_uvt_ref,
                 cw_ref, cb_ref, ln_g_ref, ln_b_ref, cog_ref, kc_ref, ks1_ref, ks2_ref,
                 qcos_ref, qsin_ref, attn_g_ref, w_out_ref, ffn_g_ref, w_up_ref,
                 halo_ref, km_ref, vm_ref, uph_ref, xbuf):
    h = jnp.concatenate(
        [meta_ref[...], jnp.zeros((META_ROWS - N_META, D_MODEL), jnp.float32)], axis=0)
    n = _rms(h, mix_g_ref[...]).astype(jnp.bfloat16)

    xbuf[:, 0:HALO, :] = jnp.zeros((N_CONV_SLABS, HALO, LANES), jnp.float32)
    u_n = _conv_group(n, w_in_ref, xbuf, cw_ref, cb_ref, ln_g_ref, ln_b_ref, cog_ref)
    halo_ref[:, 0:HALO - N_META, :] = jnp.zeros((N_CONV_SLABS, HALO - N_META, LANES), jnp.float32)
    halo_ref[:, HALO - N_META:HALO, :] = xbuf[:, HALO:HALO + N_META, :]

    k, qt, vt = _qkv(n, w_in_ref, q_g_ref, w_uqt_ref, kv_g_ref, w_uk_ref, w_uvt_ref,
                     kc_ref[...], ks1_ref[...], ks2_ref[...], qcos_ref[...], qsin_ref[...])
    km = k[0:N_META]
    key_is_meta = lax.broadcasted_iota(jnp.int32, (D_ATTN, META_ROWS), 1) < N_META
    vm = jnp.where(key_is_meta, vt, jnp.zeros_like(vt))
    km_ref[...] = km
    vm_ref[...] = vm

    _, accs = _meta_keys_softmax(km, vm, qt)
    o_n = _attn_output(accs, attn_g_ref[...])

    h1 = (h + jnp.dot(u_n, w_out_ref[0:D_CONV, :], preferred_element_type=jnp.float32)
          + jnp.dot(o_n, w_out_ref[D_CONV:, :], preferred_element_type=jnp.float32))
    n2 = _rms(h1, ffn_g_ref[...]).astype(jnp.bfloat16)
    for c in range(2 * D_FF // FF_CHUNK):
        up = jnp.dot(n2, w_up_ref[:, c * FF_CHUNK:(c + 1) * FF_CHUNK],
                     preferred_element_type=jnp.float32)
        for s in range(FF_CHUNK // LANES):
            uph_ref[c * (FF_CHUNK // LANES) + s] = up[N_META - SUBLANES:N_META,
                                                      s * LANES:(s + 1) * LANES]


def _proj_kernel(x_ref, halo_ref, mix_g_ref, w_in_ref, q_g_ref, w_uqt_ref, kv_g_ref,
                 w_uk_ref, w_uvt_ref, cw_ref, cb_ref, ln_g_ref, ln_b_ref, cog_ref,
                 kc_ref, ks1_ref, ks2_ref, qcos_ref, qsin_ref,
                 u_ref, k_ref, qt_ref, vt_ref, xbuf):
    @pl.when(pl.program_id(1) == 0)
    def _():
        xbuf[:, 0:HALO, :] = halo_ref[...]

    n = _rms(x_ref[0], mix_g_ref[...]).astype(jnp.bfloat16)
    u_ref[0] = _conv_group(n, w_in_ref, xbuf, cw_ref, cb_ref, ln_g_ref, ln_b_ref, cog_ref)
    k, qt, vt = _qkv(n, w_in_ref, q_g_ref, w_uqt_ref, kv_g_ref, w_uk_ref, w_uvt_ref,
                     kc_ref[...], ks1_ref[...], ks2_ref[...], qcos_ref[...], qsin_ref[...])
    for j in range(T_PROJ // T):
        k_ref[0, j] = pltpu.bitcast(k[j * T:(j + 1) * T], jnp.int32)
        qt_ref[0, j] = qt[:, j * T:(j + 1) * T]
        vt_ref[0, j] = pltpu.bitcast(vt[:, j * T:(j + 1) * T], jnp.int32)


def _attn_kernel(qt_ref, qtn_ref, k_ref, vt_ref, km_ref, vm_ref, g_ref, o_ref,
                 m_sc, acc_sc, alpha_sc, smax_sc, s_sc, p_sc):
    i = pl.program_id(1)
    heads = _head_slices()
    lanes = [(qh, qh * N_HEADS + hd, qsl, vsl, slice((qh * N_HEADS + hd) * V_AUG,
                                                   (qh * N_HEADS + hd + 1) * V_AUG))
             for qh in range(Q_TILES) for hd, (qsl, vsl, _) in enumerate(heads)]
    FULL, DIAG, SKIP = "full", "diag", "skip"
    all_full = [FULL] * len(lanes)
    tile_a = [DIAG if qh == 0 else FULL for qh, *_ in lanes]
    tile_b = [SKIP if qh == 0 else DIAG for qh, *_ in lanes]

    for qh in range(Q_TILES):
        maxes, accs = _meta_keys_softmax(km_ref[...], vm_ref[...], qt_ref[0, qh])
        for hd in range(N_HEADS):
            _, ln, _, _, asl = lanes[qh * N_HEADS + hd]
            m_sc[ln] = maxes[hd]
            acc_sc[asl, :] = accs[hd]

    def scores(n, slot, modes, q_ref=qt_ref):
        for (qh, ln, qsl, _, _), mode in zip(lanes, modes):
            if mode == SKIP:
                continue
            s = jnp.dot(pltpu.bitcast(k_ref[0, n - 1, :, qsl], jnp.bfloat16), q_ref[0, qh, qsl, :],
                        preferred_element_type=jnp.float32)
            s_sc[slot, ln] = s
            smax_sc[slot, ln] = jnp.max(s, axis=0, keepdims=True)

    def softmax(slot, modes):
        if DIAG in modes:
            key_chunk = lax.broadcasted_iota(jnp.int32, (T, T), 0) // CHUNK
            qry_chunk = lax.broadcasted_iota(jnp.int32, (T, T), 1) // CHUNK
            visible = key_chunk <= qry_chunk
        for (_, ln, _, _, _), mode in zip(lanes, modes):
            if mode == SKIP:
                continue
            s = s_sc[slot, ln]
            if mode == DIAG:
                s = jnp.where(visible, s, NEG)
                tile_max = jnp.max(s, axis=0, keepdims=True)
            else:
                tile_max = smax_sc[slot, ln]
            m = m_sc[ln]
            m_new = jnp.maximum(m, tile_max)
            alpha_sc[slot, ln] = jnp.exp2(m - m_new)
            p_sc[slot, ln] = jnp.exp2(s - m_new).astype(jnp.bfloat16)
            m_sc[ln] = m_new

    def values(n, slot, modes):
        for (_, ln, _, vsl, asl), mode in zip(lanes, modes):
            if mode == SKIP:
                continue
            vt = pltpu.bitcast(
                vt_ref[0, jnp.maximum(n - 1, 0), vsl.start // 2:vsl.stop // 2, :], jnp.bfloat16)
            acc_sc[asl, :] = alpha_sc[slot, ln] * acc_sc[asl, :] + jnp.dot(
                _with_ones(vt), p_sc[slot, ln], preferred_element_type=jnp.float32)

    def stage(n, slot, modes, prev_modes, next_modes):
        for qsel in range(Q_TILES):
            sel = lambda ms: [m if qh == qsel else SKIP for (qh, *_), m in zip(lanes, ms)]
            softmax(slot, sel(modes))
            if next_modes is not None:
                scores(n + 1, 1 - slot, sel(next_modes))
            values(n - 1, 1 - slot, sel(prev_modes))

    p_sc[0] = jnp.zeros((len(lanes), T, T), jnp.bfloat16)
    alpha_sc[0] = jnp.ones((len(lanes), 1, T), jnp.float32)

    @pl.when(i == 0)
    def _():
        scores(1, 1, all_full)

    def pair(kk, carry):
        n = 2 * kk + 1
        stage(n, 1, all_full, all_full, all_full)
        stage(n + 1, 0, all_full, all_full, all_full)
        return carry

    lax.fori_loop(0, i, pair, 0)
    n_a = 2 * i + 1
    stage(n_a, 1, tile_a, all_full, tile_b)
    stage(n_a + 1, 0, tile_b, tile_a, None)
    values(n_a + 1, 0, tile_b)
    scores(1, 1, all_full, q_ref=qtn_ref)

    for qh in range(Q_TILES):
        o_ref[0, qh * T:(qh + 1) * T, :] = _attn_output(
            [acc_sc[asl, :] for q, _, _, _, asl in lanes if q == qh], g_ref[...])


def _ffn_kernel(x_ref, uph_ref, u_ref, o_ref, w_out_ref, ffn_g_ref, w_up_ref, fcw_ref,
                fcb_ref, w_down_ref, fin_g_ref, out_ref, upbuf, acc_ref):
    t = pl.program_id(1)
    h1 = (x_ref[0]
          + jnp.dot(u_ref[0], w_out_ref[0:D_CONV, :], preferred_element_type=jnp.float32)
          + jnp.dot(o_ref[0], w_out_ref[D_CONV:, :], preferred_element_type=jnp.float32))
    acc_ref[...] = h1
    n2 = _rms(h1, ffn_g_ref[...]).astype(jnp.bfloat16)

    @pl.when(t == 0)
    def _():
        upbuf[:, 0:SUBLANES, :] = uph_ref[...]

    def up_proj(c):
        return tuple(
            jnp.dot(n2, w_up_ref[:, col0:col0 + FF_CHUNK], preferred_element_type=jnp.float32)
            for col0 in (c * FF_CHUNK, D_FF + c * FF_CHUNK))

    def conv3(up, col0):
        outs = []
        for s in range(FF_CHUNK // LANES):
            slab = col0 // LANES + s
            cols = slice(col0 + s * LANES, col0 + (s + 1) * LANES)
            upbuf[slab, SUBLANES:SUBLANES + T_FFN, :] = up[:, s * LANES:(s + 1) * LANES]
            y = fcb_ref[:, cols] + jnp.zeros((T_FFN, LANES), jnp.float32)
            for k in range(FFN_CONV_WIDTH):
                off = SUBLANES - (FFN_CONV_WIDTH - 1) + k
                y = y + fcw_ref[k:k + 1, cols] * upbuf[slab, off:off + T_FFN, :]
            outs.append(y)
            upbuf[slab, 0:SUBLANES, :] = upbuf[slab, T_FFN:T_FFN + SUBLANES, :]
        return jnp.concatenate(outs, axis=-1)

    ups = [up_proj(c) for c in range(FF_AHEAD)]
    for c in range(N_FF_CHUNKS):
        if c + FF_AHEAD < N_FF_CHUNKS:
            ups.append(up_proj(c + FF_AHEAD))
        up_g, up_val = ups.pop(0)
        g = conv3(up_g, c * FF_CHUNK)
        val = conv3(up_val, D_FF + c * FF_CHUNK)
        act = (g * jax.nn.sigmoid(g) * val).astype(jnp.bfloat16)
        contrib = jnp.dot(act, w_down_ref[c * FF_CHUNK:(c + 1) * FF_CHUNK, :],
                          preferred_element_type=jnp.float32)
        acc_ref[...] += contrib
    out_ref[0] = _rms(acc_ref[...], fin_g_ref[...]).astype(out_ref.dtype)


def _full(shape):
    return pl.BlockSpec(shape, lambda *_: (0,) * len(shape))


def _rope_tables(pos):
    inv_freq = 1.0 / (ROPE_THETA ** (np.arange(0, QK_ROPE, 2, dtype=np.float64) / QK_ROPE))
    ang = np.asarray(pos, np.float64)[:, None] * inv_freq[None, :]
    cos, sin = np.cos(ang), np.sin(ang)
    zl = lambda n: np.zeros((len(pos), n))
    kc = np.concatenate([zl(QK_NOPE), cos, cos, zl(LANES - QK_NOPE - QK_ROPE)], axis=1)
    ks1 = np.concatenate([zl(QK_NOPE + ROPE_HALF), sin, zl(LANES - QK_NOPE - QK_ROPE)], axis=1)
    ks2 = np.concatenate([zl(QK_NOPE), -sin, zl(LANES - QK_NOPE - ROPE_HALF)], axis=1)
    tabs = (kc, ks1, ks2, (cos * Q_SCALE).T, (sin * Q_SCALE).T)
    return tuple(jnp.asarray(t.astype(np.float32)) for t in tabs)


def kernel(x, meta_tokens, mix_norm_g, w_in, q_norm_g, w_uq, kv_norm_g, w_ukv, conv_w, conv_b,
           conv_ln_g, conv_ln_b, conv_out_g, attn_out_g, w_out, ffn_norm_g, w_ffn_up,
           ffn_conv_w, ffn_conv_b, w_ffn_down, final_norm_g):
    B, S, D = x.shape
    assert D == D_MODEL and S % (Q_TILES * T) == 0 and S % T_FFN == 0 and mix_norm_g.shape[0] == 1
    assert meta_tokens.shape == (N_META, D)
    nt = S // T
    bf16, f32 = jnp.bfloat16, jnp.float32
    row2 = lambda v: v.reshape(1, -1).astype(f32)

    w_in_p = jnp.pad(w_in[0].astype(bf16), ((0, 0), (0, LANES - QK_ROPE)))
    w_uq3 = w_uq[0].reshape(Q_LORA, N_HEADS, QK_NOPE + QK_ROPE)
    w_uqt = jnp.pad(w_uq3, ((0, 0), (0, 0), (0, HEAD_PAD - QK_NOPE - QK_ROPE))).reshape(
        Q_LORA, N_HEADS * HEAD_PAD).T.astype(bf16)
    w_ukv3 = w_ukv[0].reshape(KV_LORA, N_HEADS, QK_NOPE + V_HEAD)
    w_uk = jnp.pad(w_ukv3[:, :, :QK_NOPE], ((0, 0), (0, 0), (0, HEAD_PAD - QK_NOPE))).reshape(
        KV_LORA, N_HEADS * HEAD_PAD).astype(bf16)
    w_uvt = w_ukv3[:, :, QK_NOPE:].reshape(KV_LORA, D_ATTN).T.astype(bf16)
    w_out_b = w_out[0].astype(bf16)
    w_up_b = w_ffn_up[0].astype(bf16)
    w_down_b = w_ffn_down[0].astype(bf16)
    mix_g, q_g, kv_g = row2(mix_norm_g[0]), row2(q_norm_g[0]), row2(kv_norm_g[0])
    cw, cb = conv_w[0].astype(f32), row2(conv_b[0])
    ln_g, ln_b, cog = row2(conv_ln_g[0]), row2(conv_ln_b[0]), row2(conv_out_g[0])
    attn_g, ffn_g = row2(attn_out_g[0]), row2(ffn_norm_g[0])

    meta_tabs = _rope_tables(np.arange(META_ROWS))
    tabs = _rope_tables(np.arange(S) + N_META)

    proj_w = [mix_g, w_in_p, q_g, w_uqt, kv_g, w_uk, w_uvt]
    proj_specs = [_full((1, D)), _full((D, D_IN_PAD)), _full((1, Q_LORA)),
                  _full((N_HEADS * HEAD_PAD, Q_LORA)), _full((1, KV_LORA)),
                  _full((KV_LORA, N_HEADS * HEAD_PAD)), _full((D_ATTN, KV_LORA))]
    conv_w_ = [cw, cb, ln_g, ln_b, cog]
    conv_specs = [_full((CONV_WIDTH, D_CONV)), _full((1, D_CONV)), _full((1, D_CONV)),
                  _full((1, D_CONV)), _full((1, D_CONV))]
    front_w, front_specs = proj_w + conv_w_, proj_specs + conv_specs
    seq = ("arbitrary", "arbitrary")

    conv_halo, k_meta, v_meta, up_halo = pl.pallas_call(
        _meta_kernel,
        grid=(1,),
        in_specs=[_full((N_META, D))] + front_specs + [
            _full((META_ROWS, LANES)), _full((META_ROWS, LANES)), _full((META_ROWS, LANES)),
            _full((ROPE_HALF, META_ROWS)), _full((ROPE_HALF, META_ROWS)),
            _full((1, D_ATTN)), _full((D_CONV + D_ATTN, D)), _full((1, D)), _full((D, 2 * D_FF))],
        out_specs=[_full((N_CONV_SLABS, HALO, LANES)), _full((N_META, N_HEADS * HEAD_PAD)),
                   _full((D_ATTN, META_ROWS)), _full((N_FF_SLABS, SUBLANES, LANES))],
        out_shape=[jax.ShapeDtypeStruct((N_CONV_SLABS, HALO, LANES), f32),
                   jax.ShapeDtypeStruct((N_META, N_HEADS * HEAD_PAD), bf16),
                   jax.ShapeDtypeStruct((D_ATTN, META_ROWS), bf16),
                   jax.ShapeDtypeStruct((N_FF_SLABS, SUBLANES, LANES), f32)],
        scratch_shapes=[pltpu.VMEM((N_CONV_SLABS, HALO + META_ROWS, LANES), f32)],
        compiler_params=pltpu.CompilerParams(dimension_semantics=("arbitrary",),
                                             vmem_limit_bytes=VMEM_LIMIT),
        name="meta_tokens",
    )(meta_tokens.astype(f32), *front_w, *meta_tabs, attn_g, w_out_b, ffn_g, w_up_b)

    params = pltpu.CompilerParams(dimension_semantics=seq, vmem_limit_bytes=VMEM_LIMIT)
    u_n, k4, qt4, vt4 = pl.pallas_call(
        _proj_kernel,
        grid=(B, S // T_PROJ),
        in_specs=[pl.BlockSpec((1, T_PROJ, D), lambda b, t: (b, t, 0)),
                  _full((N_CONV_SLABS, HALO, LANES))] + front_specs + [
            pl.BlockSpec((T_PROJ, LANES), lambda b, t: (t, 0)),
            pl.BlockSpec((T_PROJ, LANES), lambda b, t: (t, 0)),
            pl.BlockSpec((T_PROJ, LANES), lambda b, t: (t, 0)),
            pl.BlockSpec((ROPE_HALF, T_PROJ), lambda b, t: (0, t)),
            pl.BlockSpec((ROPE_HALF, T_PROJ), lambda b, t: (0, t))],
        out_specs=[
            pl.BlockSpec((1, T_PROJ, D_CONV), lambda b, t: (b, t, 0)),
            pl.BlockSpec((1, T_PROJ // T, T // 2, N_HEADS * HEAD_PAD), lambda b, t: (b, t, 0, 0)),
            pl.BlockSpec((1, T_PROJ // T, N_HEADS * HEAD_PAD, T), lambda b, t: (b, t, 0, 0)),
            pl.BlockSpec((1, T_PROJ // T, D_ATTN // 2, T), lambda b, t: (b, t, 0, 0)),
        ],
        out_shape=[
            jax.ShapeDtypeStruct((B, S, D_CONV), bf16),
            jax.ShapeDtypeStruct((B, nt, T // 2, N_HEADS * HEAD_PAD), jnp.int32),
            jax.ShapeDtypeStruct((B, nt, N_HEADS * HEAD_PAD, T), bf16),
            jax.ShapeDtypeStruct((B, nt, D_ATTN // 2, T), jnp.int32),
        ],
        scratch_shapes=[pltpu.VMEM((N_CONV_SLABS, HALO + T_PROJ, LANES), f32)],
        compiler_params=params,
        name="proj_conv_qkv",
    )(x, conv_halo, *front_w, *tabs)

    n_lanes = Q_TILES * N_HEADS
    o_n = pl.pallas_call(
        _attn_kernel,
        grid=(B, nt // Q_TILES),
        in_specs=[
            pl.BlockSpec((1, Q_TILES, N_HEADS * HEAD_PAD, T), lambda b, i: (b, i, 0, 0)),
            pl.BlockSpec((1, Q_TILES, N_HEADS * HEAD_PAD, T),
                         lambda b, i: (b, jnp.minimum(i + 1, nt // Q_TILES - 1), 0, 0)),
            pl.BlockSpec((1, nt, T // 2, N_HEADS * HEAD_PAD), lambda b, i: (b, 0, 0, 0)),
            pl.BlockSpec((1, nt, D_ATTN // 2, T), lambda b, i: (b, 0, 0, 0)),
            _full((N_META, N_HEADS * HEAD_PAD)), _full((D_ATTN, META_ROWS)), _full((1, D_ATTN)),
        ],
        out_specs=pl.BlockSpec((1, Q_TILES * T, D_ATTN), lambda b, i: (b, i, 0)),
        out_shape=jax.ShapeDtypeStruct((B, S, D_ATTN), bf16),
        scratch_shapes=[pltpu.VMEM((n_lanes, 1, T), f32),
                        pltpu.VMEM((n_lanes * V_AUG, T), f32),
                        pltpu.VMEM((2, n_lanes, 1, T), f32),
                        pltpu.VMEM((2, n_lanes, 1, T), f32),
                        pltpu.VMEM((2, n_lanes, T, T), f32),
                        pltpu.VMEM((2, n_lanes, T, T), bf16)],
        compiler_params=params,
        name="block_causal_attn",
    )(qt4, qt4, k4, vt4, k_meta, v_meta, attn_g)

    nf = S // T_FFN
    out = pl.pallas_call(
        _ffn_kernel,
        grid=(B, nf),
        in_specs=[
            pl.BlockSpec((1, T_FFN, D), lambda b, t: (b, t, 0)),
            _full((N_FF_SLABS, SUBLANES, LANES)),
            pl.BlockSpec((1, T_FFN, D_CONV), lambda b, t: (b, t, 0)),
            pl.BlockSpec((1, T_FFN, D_ATTN), lambda b, t: (b, t, 0)),
            _full((D_CONV + D_ATTN, D)), _full((1, D)), _full((D, 2 * D_FF)),
            _full((FFN_CONV_WIDTH, 2 * D_FF)), _full((1, 2 * D_FF)), _full((D_FF, D)),
            _full((1, D)),
        ],
        out_specs=pl.BlockSpec((1, T_FFN, D), lambda b, t: (b, t, 0)),
        out_shape=jax.ShapeDtypeStruct((B, S, D), x.dtype),
        scratch_shapes=[pltpu.VMEM((N_FF_SLABS, SUBLANES + T_FFN, LANES), f32),
                        pltpu.VMEM((T_FFN, D), f32)],
        compiler_params=params,
        name="outproj_convffn",
    )(x, up_halo, u_n, o_n, w_out_b, ffn_g, w_up_b, ffn_conv_w[0].astype(f32),
      row2(ffn_conv_b[0]), w_down_b, row2(final_norm_g))
    return out
```
